```python
import jax, jax.numpy as jnp
from jax import lax
import numpy as np

D_MODEL = 1024
BATCH = 32
SEQ = 256
DEPTH = 2
DEC_BATCH = 4
DEC_SEQ = 4096
PAST_LEN = 256

GRID_W = 64
HEAD_DIM = 64
N_Q_HEADS = 8
N_KV_HEADS = 2
N_GROUPS = N_Q_HEADS // N_KV_HEADS
ATTN_WIDTH = N_Q_HEADS * HEAD_DIM
KV_WIDTH = N_KV_HEADS * HEAD_DIM
N_DN_HEADS = 8
DK = 64
DV = 64
DN_K_WIDTH = N_DN_HEADS * DK
DN_V_WIDTH = N_DN_HEADS * DV
CONV_W = 3
CONV_CH = 2 * DN_K_WIDTH + DN_V_WIDTH
CHUNK = 64
Q_BLOCK = 128
D_FF = 4 * D_MODEL
ROPE_THETA = 10000.0
ROPE_FREQS = HEAD_DIM // 4
EPS = 1e-6
SPLIT_SIZES = (ATTN_WIDTH, KV_WIDTH, KV_WIDTH, DN_K_WIDTH, DN_K_WIDTH, DN_V_WIDTH, DN_V_WIDTH,
               2 * N_DN_HEADS, 2 * N_DN_HEADS, D_MODEL, D_MODEL)
IN_WIDTH = sum(SPLIT_SIZES)

kernel_name = 'hybrid_gqa_deltanet_diffusion_step'


def rmsnorm(x, g):
    xf = x.astype(jnp.float32)
    y = xf * lax.rsqrt(jnp.mean(xf * xf, axis=-1, keepdims=True) + EPS)
    return (y * g.astype(jnp.float32)).astype(x.dtype)


def l2norm(x):
    xf = x.astype(jnp.float32)
    return xf * lax.rsqrt(jnp.sum(xf * xf, axis=-1, keepdims=True) + EPS)


def adaln(c, w_mod, b_mod):
    return jax.nn.silu(c) @ w_mod + b_mod


def axial_rope(n_tokens):
    rows = n_tokens // GRID_W
    t = jnp.arange(rows * GRID_W)
    row = (t // GRID_W).astype(jnp.float32)
    col = (t % GRID_W).astype(jnp.float32)
    inv = ROPE_THETA ** (-jnp.arange(ROPE_FREQS, dtype=jnp.float32) / ROPE_FREQS)
    ang = jnp.concatenate([row[:, None] * inv, col[:, None] * inv], axis=-1)
    return jnp.cos(ang), jnp.sin(ang)


def apply_rope(x, cos, sin):
    half = x.shape[-1] // 2
    x1, x2 = x[..., :half], x[..., half:]
    cos = cos.astype(x.dtype)
    sin = sin.astype(x.dtype)
    return jnp.concatenate([x1 * cos - x2 * sin, x2 * cos + x1 * sin], axis=-1)


def short_conv(x, w):
    T = x.shape[1]
    pad = CONV_W // 2
    xp = jnp.pad(x, ((0, 0), (pad, CONV_W - 1 - pad), (0, 0)))
    y = xp[:, 0:T] * w[0]
    for j in range(1, CONV_W):
        y = y + xp[:, j:j + T] * w[j]
    return jax.nn.silu(y)


def block_attention(q, k, v):
    B, KV, G, T, hd = q.shape
    nb = T // Q_BLOCK
    qb = jnp.moveaxis(q.reshape(B, KV, G, nb, Q_BLOCK, hd), 3, 0)
    scale = HEAD_DIM ** -0.5

    def one_block(qblk):
        s = jnp.einsum('bkgqd,bkjd->bkgqj', qblk, k).astype(jnp.float32) * scale
        p = jax.nn.softmax(s, axis=-1).astype(v.dtype)
        return jnp.einsum('bkgqj,bkjd->bkgqd', p, v)

    ob = lax.map(one_block, qb)
    return jnp.moveaxis(ob, 0, 3).reshape(B, KV * G, T, hd)


def delta_chunked(q, k, v, log_a, beta, s0):
    B, H, T, dk = q.shape
    dv = v.shape[-1]
    n = T // CHUNK
    q = q.reshape(B, H, n, CHUNK, dk)
    k = k.reshape(B, H, n, CHUNK, dk)
    v = v.reshape(B, H, n, CHUNK, dv)
    g = jnp.cumsum(log_a.reshape(B, H, n, CHUNK), axis=-1)
    b = beta.reshape(B, H, n, CHUNK)
    idx = jnp.arange(CHUNK)
    incl = idx[:, None] >= idx[None, :]
    strict = idx[:, None] > idx[None, :]
    decay = jnp.exp(jnp.where(incl, g[..., :, None] - g[..., None, :], -jnp.inf))
    kk = jnp.einsum('bhnid,bhnjd->bhnij', k, k)
    lmat = jnp.where(strict, b[..., :, None] * kk * decay, 0.0) + jnp.eye(CHUNK, dtype=kk.dtype)
    w = lax.linalg.triangular_solve(lmat, (b * jnp.exp(g))[..., None] * k,
                                    left_side=True, lower=True, unit_diagonal=True)
    u = lax.linalg.triangular_solve(lmat, b[..., None] * v,
                                    left_side=True, lower=True, unit_diagonal=True)
    qk = jnp.einsum('bhnid,bhnjd->bhnij', q, k) * decay
    q_dec = q * jnp.exp(g)[..., None]
    g_last = g[..., -1]
    k_dec = k * jnp.exp(g_last[..., None] - g)[..., None]
    xs = (jnp.moveaxis(w, 2, 0), jnp.moveaxis(u, 2, 0), jnp.moveaxis(q_dec, 2, 0),
          jnp.moveaxis(qk, 2, 0), jnp.moveaxis(k_dec, 2, 0), jnp.moveaxis(jnp.exp(g_last), 2, 0))

    def step(S, inp):
        w_c, u_c, qd_c, qk_c, kd_c, eg_c = inp
        u_c = u_c - jnp.einsum('bhcd,bhde->bhce', w_c, S)
        o_c = jnp.einsum('bhcd,bhde->bhce', qd_c, S) + jnp.einsum('bhij,bhje->bhie', qk_c, u_c)
        S = eg_c[..., None, None] * S + jnp.einsum('bhcd,bhce->bhde', kd_c, u_c)
        return S, o_c

    S, o = lax.scan(step, s0, xs)
    return jnp.moveaxis(o, 0, 2).reshape(B, H, T, dv), S


def trunk_layer(x, mod, w_in, conv_w, q_gain, k_gain, a_log, dt_bias, dn_gain, w_pa, w_pd, w_out,
                norm1, norm2, w1, b1, w2, b2, ctx, rope):
    B, T, _ = x.shape
    shift1, scale1, gate1, shift2, scale2, gate2 = jnp.split(mod.astype(x.dtype), 6, axis=-1)
    h = rmsnorm(x, norm1) * (1 + scale1) + shift1
    split_idx = [int(i) for i in np.cumsum(SPLIT_SIZES)[:-1]]
    (q_a, k_a, v_a, q_d, k_d, v_d, g_out, a_in, beta_in, gate_a, gate_d) = jnp.split(
        h @ w_in, split_idx, axis=-1)
    q_a = rmsnorm(q_a.reshape(B, T, N_Q_HEADS, HEAD_DIM), q_gain).transpose(0, 2, 1, 3)
    k_a = rmsnorm(k_a.reshape(B, T, N_KV_HEADS, HEAD_DIM), k_gain).transpose(0, 2, 1, 3)
    v_a = v_a.reshape(B, T, N_KV_HEADS, HEAD_DIM).transpose(0, 2, 1, 3)
    qkv_d = short_conv(jnp.concatenate([q_d, k_d, v_d], axis=-1), conv_w)
    q_d, k_d, v_d = jnp.split(qkv_d, [DN_K_WIDTH, 2 * DN_K_WIDTH], axis=-1)
    q_d = (l2norm(q_d.reshape(B, T, N_DN_HEADS, DK)) * (DK ** -0.5)).transpose(0, 2, 1, 3)
    k_d = l2norm(k_d.reshape(B, T, N_DN_HEADS, DK)).transpose(0, 2, 1, 3)
    v_d = v_d.reshape(B, T, N_DN_HEADS, DV).astype(jnp.float32).transpose(0, 2, 1, 3)
    a_in = a_in.reshape(B, T, 2, N_DN_HEADS).astype(jnp.float32)
    log_a = (-jnp.exp(a_log.astype(jnp.float32)) *
             jax.nn.softplus(a_in + dt_bias.astype(jnp.float32))).transpose(0, 2, 3, 1)
    beta = jax.nn.sigmoid(beta_in.astype(jnp.float32)).reshape(B, T, 2, N_DN_HEADS).transpose(0, 2, 3, 1)

    if ctx is None:
        k_all, v_all = k_a, v_a
        s_f = jnp.zeros((B, N_DN_HEADS, DK, DV), jnp.float32)
        s_b = jnp.zeros((B, N_DN_HEADS, DK, DV), jnp.float32)
        q_use = q_a
    else:
        k_ctx, v_ctx, s_f, s_b = ctx
        cos, sin = rope
        q_use = apply_rope(q_a, cos, sin)
        k_all = jnp.concatenate([k_ctx.astype(x.dtype), apply_rope(k_a, cos, sin)], axis=2)
        v_all = jnp.concatenate([v_ctx.astype(x.dtype), v_a], axis=2)
        s_f = s_f.astype(jnp.float32)
        s_b = s_b.astype(jnp.float32)

    o_a = block_attention(q_use.reshape(B, N_KV_HEADS, N_GROUPS, T, HEAD_DIM), k_all, v_all)
    o_a = o_a.transpose(0, 2, 1, 3).reshape(B, T, ATTN_WIDTH)

    o_f, S_f = delta_chunked(q_d, k_d, v_d, log_a[:, 0], beta[:, 0], s_f)
    o_b, S_b = delta_chunked(jnp.flip(q_d, 2), jnp.flip(k_d, 2), jnp.flip(v_d, 2),
                             jnp.flip(log_a[:, 1], -1), jnp.flip(beta[:, 1], -1), s_b)
    o_d = (o_f + jnp.flip(o_b, 2)).transpose(0, 2, 1, 3)
    o_d = rmsnorm(o_d, dn_gain) * jax.nn.silu(g_out.reshape(B, T, N_DN_HEADS, DV).astype(jnp.float32))
    o_d = o_d.reshape(B, T, DN_V_WIDTH).astype(x.dtype)

    merged = jax.nn.sigmoid(gate_a) * (o_a @ w_pa) + jax.nn.sigmoid(gate_d) * (o_d @ w_pd)
    x = x + gate1 * (merged @ w_out)
    h2 = rmsnorm(x, norm2) * (1 + scale2) + shift2
    ff = jnp.square(jax.nn.relu(h2 @ w1 + b1)) @ w2 + b2
    x = x + gate2 * ff
    return x, k_a, v_a, S_f, S_b


def setup_inputs(seed: int = 0) -> dict:
    key = jax.random.key(seed)
    ks = jax.random.split(key, 32)

    def nrm(k, shape, scale):
        return jax.random.normal(k, shape, jnp.float32) * scale

    dt = jnp.exp(jax.random.uniform(ks[14], (DEPTH, 2, N_DN_HEADS), jnp.float32,
                                    float(np.log(1e-3)), float(np.log(1e-1))))
    return {
        'x_prompt': nrm(ks[0], (BATCH, SEQ, D_MODEL), 1.0),
        'x_sample': nrm(ks[1], (DEC_BATCH, DEC_SEQ, D_MODEL), 1.0),
        'c': nrm(ks[2], (DEC_BATCH, D_MODEL), 1.0),
        'cache_k': nrm(ks[3], (DEC_BATCH, DEPTH, N_KV_HEADS, PAST_LEN, HEAD_DIM), 1.0),
        'cache_v': nrm(ks[4], (DEC_BATCH, DEPTH, N_KV_HEADS, PAST_LEN, HEAD_DIM), 1.0),
        'state_delta': nrm(ks[5], (DEC_BATCH, DEPTH, 2, N_DN_HEADS, DK, DV), 0.1),
        'c_ctx': nrm(ks[6], (D_MODEL,), 1.0),
        'w_mod': nrm(ks[7], (DEPTH, D_MODEL, 6 * D_MODEL), 0.5 * D_MODEL ** -0.5),
        'b_mod': nrm(ks[8], (DEPTH, 6 * D_MODEL), 0.02),
        'norm1': 1.0 + nrm(ks[9], (DEPTH, D_MODEL), 0.02),
        'norm2': 1.0 + nrm(ks[10], (DEPTH, D_MODEL), 0.02),
        'w_in': nrm(ks[11], (DEPTH, D_MODEL, IN_WIDTH), D_MODEL ** -0.5),
        'conv_w': nrm(ks[12], (DEPTH, CONV_W, CONV_CH), CONV_W ** -0.5),
        'q_gain': 1.0 + nrm(ks[13], (DEPTH, HEAD_DIM), 0.02),
        'k_gain': 1.0 + nrm(ks[15], (DEPTH, HEAD_DIM), 0.02),
        'a_log': jnp.log(jax.random.uniform(ks[16], (DEPTH, 2, N_DN_HEADS), jnp.float32, 1.0, 16.0)),
        'dt_bias': dt + jnp.log(-jnp.expm1(-dt)),
        'dn_gain': 1.0 + nrm(ks[17], (DEPTH, DV), 0.02),
        'w_pa': nrm(ks[18], (DEPTH, ATTN_WIDTH, D_MODEL), ATTN_WIDTH ** -0.5),
        'w_pd': nrm(ks[19], (DEPTH, DN_V_WIDTH, D_MODEL), DN_V_WIDTH ** -0.5),
        'w_out': nrm(ks[20], (DEPTH, D_MODEL, D_MODEL), D_MODEL ** -0.5),
        'w1': nrm(ks[21], (DEPTH, D_MODEL, D_FF), D_MODEL ** -0.5),
        'b1': nrm(ks[22], (DEPTH, D_FF), 0.02),
        'w2': nrm(ks[23], (DEPTH, D_FF, D_MODEL), D_FF ** -0.5),
        'b2': nrm(ks[24], (DEPTH, D_MODEL), 0.02),
        'final_norm': 1.0 + nrm(ks[25], (D_MODEL,), 0.02),
    }


def reference(x_prompt, x_sample, c, cache_k, cache_v, state_delta, c_ctx, w_mod, b_mod, norm1, norm2,
              w_in, conv_w, q_gain, k_gain, a_log, dt_bias, dn_gain, w_pa, w_pd, w_out,
              w1, b1, w2, b2, final_norm):
    rope = axial_rope(x_sample.shape[1])
    xp, xs = x_prompt, x_sample
    ks_out, vs_out, st_out = [], [], []
    for l in range(DEPTH):
        lw = (w_in[l], conv_w[l], q_gain[l], k_gain[l], a_log[l], dt_bias[l], dn_gain[l],
              w_pa[l], w_pd[l], w_out[l], norm1[l], norm2[l], w1[l], b1[l], w2[l], b2[l])
        mod_ctx = adaln(c_ctx, w_mod[l], b_mod[l])[None, None, :]
        xp, k_c, v_c, s_fwd, s_bwd = trunk_layer(xp, mod_ctx, *lw, ctx=None, rope=None)
        ks_out.append(k_c)
        vs_out.append(v_c)
        st_out.append(jnp.stack([s_fwd, s_bwd], axis=1))
        mod_lat = adaln(c, w_mod[l], b_mod[l])[:, None, :]
        ctx = (cache_k[:, l], cache_v[:, l], state_delta[:, l, 0], state_delta[:, l, 1])
        xs, _, _, _, _ = trunk_layer(xs, mod_lat, *lw, ctx=ctx, rope=rope)
    y_prompt = rmsnorm(xp, final_norm)
    y_sample = rmsnorm(xs, final_norm)
    new_cache_k = jnp.stack(ks_out, axis=1)
    new_cache_v = jnp.stack(vs_out, axis=1)
    new_state_delta = jnp.stack(st_out, axis=1)
    return (y_prompt, y_sample, new_cache_k, new_cache_v, new_state_delta)
```

```python
import functools

import jax
import jax.numpy as jnp
from jax import lax
from jax.experimental import pallas as pl
from jax.experimental.pallas import tpu as pltpu

F32 = jnp.float32
BF16 = jnp.bfloat16

D_MODEL = 1024
HEAD = 64
N_Q_HEADS = 8
N_KV_HEADS = 2
N_DN_HEADS = 8
ATTN_W = N_Q_HEADS * HEAD
KV_W = N_KV_HEADS * HEAD
DN_W = N_DN_HEADS * HEAD
CHUNK = 64
GRID_W = 64
D_FF = 4 * D_MODEL
ROPE_THETA = 10000.0
EPS = 1e-6

LANES = 128
GROUP_W = 256
SEGS = GROUP_W // HEAD
TOK_TILE = 256
VMEM_LIMIT = 56 * 1024 * 1024

C_A = 0
C_D = ATTN_W + 2 * KV_W
C_AB = C_D + 4 * DN_W
C_G = C_AB + LANES
C_END = C_G + 2 * D_MODEL


def _sigmoid(x):
    return 1.0 / (1.0 + jnp.exp(-x))


def _bdot(a, b):
    return jnp.dot(a.astype(BF16), b.astype(BF16), preferred_element_type=F32)


def _hdot(a, b):
    return jnp.dot(a, b, preferred_element_type=F32, precision=lax.Precision.HIGHEST)


def _seg_sumsq(x, ones_bd):
    x2 = x * x
    hi = x2.astype(BF16)
    lo = (x2 - hi.astype(F32)).astype(BF16)
    width = x.shape[1]
    if width == LANES:
        m = ones_bd[:LANES, :LANES]
        return (jnp.dot(hi, m, preferred_element_type=F32)
                + jnp.dot(lo, m, preferred_element_type=F32))
    outs = []
    for g in range(width // GROUP_W):
        sl = slice(g * GROUP_W, (g + 1) * GROUP_W)
        outs.append(jnp.dot(hi[:, sl], ones_bd, preferred_element_type=F32)
                    + jnp.dot(lo[:, sl], ones_bd, preferred_element_type=F32))
    return jnp.concatenate(outs, axis=1)


def _rms_rows(x):
    return x * lax.rsqrt(jnp.mean(x * x, axis=-1, keepdims=True) + EPS)


def _mod_kernel(c_ref, w_ref, b_ref, o_ref):
    c = c_ref[...]
    s = c * _sigmoid(c)
    o_ref[0] = jnp.dot(s.astype(BF16), w_ref[0], preferred_element_type=F32) + b_ref[0]


def _modulation(cvec, w_mod_bf, b_mod):
    depth = w_mod_bf.shape[0]
    rows = cvec.shape[0]
    nblk = 6 * D_MODEL // D_MODEL
    return pl.pallas_call(
        _mod_kernel,
        grid=(depth, nblk),
        in_specs=[
            pl.BlockSpec((rows, D_MODEL), lambda l, j: (0, 0)),
            pl.BlockSpec((1, D_MODEL, D_MODEL), lambda l, j: (l, 0, j)),
            pl.BlockSpec((1, 1, D_MODEL), lambda l, j: (l, 0, j)),
        ],
        out_specs=pl.BlockSpec((1, rows, D_MODEL), lambda l, j: (l, 0, j)),
        out_shape=jax.ShapeDtypeStruct((depth, rows, 6 * D_MODEL), F32),
        name="adaln_mod",
    )(cvec, w_mod_bf, b_mod.reshape(depth, 1, 6 * D_MODEL))


def _rope(x, cos, sin_signed):
    width = x.shape[1]
    lane = lax.broadcasted_iota(jnp.int32, x.shape, 1)
    first_half = (lane % HEAD) < (HEAD // 2)
    rot = jnp.where(first_half, pltpu.roll(x, width - HEAD // 2, 1), pltpu.roll(x, HEAD // 2, 1))
    return x * cos + rot * sin_signed


def _inproj_kernel(*refs, rope, cache_out):
    (x_ref, mod_ref, n1_ref, w_ref, qg_ref, kg_ref, alog_ref, dtb_ref, ones_ref) = refs[:9]
    pos = 9
    if rope:
        cos_ref, sin_ref = refs[pos:pos + 2]
        pos += 2
    q_ref, kv_ref, d_ref, ab_ref, g_ref = refs[pos:pos + 5]
    pos += 5
    if cache_out:
        kc_ref, vc_ref = refs[pos:pos + 2]

    shift1 = mod_ref[0, :, 0:D_MODEL]
    scale1 = mod_ref[0, :, D_MODEL:2 * D_MODEL]
    h = _rms_rows(x_ref[...]) * n1_ref[...]
    h = h * (1.0 + scale1) + shift1
    y = jnp.dot(h.astype(BF16), w_ref[...], preferred_element_type=F32)
    ones_bd = ones_ref[...]

    qa = y[:, C_A:C_A + ATTN_W]
    qn = qa * lax.rsqrt(_seg_sumsq(qa, ones_bd) * (1.0 / HEAD) + EPS) * qg_ref[...]
    ka = y[:, C_A + ATTN_W:C_A + ATTN_W + KV_W]
    kn = ka * lax.rsqrt(_seg_sumsq(ka, ones_bd) * (1.0 / HEAD) + EPS) * kg_ref[...]
    va = y[:, C_A + ATTN_W + KV_W:C_D]
    if cache_out:
        kc_ref[0, 0] = kn[:, :HEAD]
        kc_ref[0, 1] = kn[:, HEAD:]
        vc_ref[0, 0] = va[:, :HEAD]
        vc_ref[0, 1] = va[:, HEAD:]
    if rope:
        cos = cos_ref[...]
        sin = sin_ref[...]
        qn = _rope(qn, jnp.concatenate([cos] * (ATTN_W // LANES), axis=1),
                   jnp.concatenate([sin] * (ATTN_W // LANES), axis=1))
        kn = _rope(kn, cos, sin)
    q_ref[...] = (qn * (HEAD ** -0.5)).astype(BF16)
    kv_ref[:, 0:LANES] = kn.astype(BF16)
    kv_ref[:, LANES:2 * LANES] = pltpu.roll(kn, HEAD, 1).astype(BF16)
    kv_ref[:, 2 * LANES:3 * LANES] = va.astype(BF16)
    kv_ref[:, 3 * LANES:4 * LANES] = pltpu.roll(va, HEAD, 1).astype(BF16)

    d_ref[...] = y[:, C_D:C_AB]

    z = y[:, C_AB:C_G]
    zb = z + dtb_ref[...]
    softplus = jnp.maximum(zb, 0.0) + jnp.log(1.0 + jnp.exp(-jnp.abs(zb)))
    log_a = -jnp.exp(alog_ref[...]) * softplus
    lane = lax.broadcasted_iota(jnp.int32, z.shape, 1)
    ab_ref[...] = jnp.where(lane < 2 * N_DN_HEADS, log_a,
                            jnp.where(lane < 4 * N_DN_HEADS, _sigmoid(z), 0.0))

    g_ref[...] = _sigmoid(y[:, C_G:C_END])


def _inproj(x2d, mod, tiles_per_mod, n1, w_cat, qg, kg, alog, dtb, ones_bd, rope_tabs, cache_out,
            seq_len):
    ntok = x2d.shape[0]
    ntiles = ntok // TOK_TILE
    rope = rope_tabs is not None
    const = lambda i: (0, 0)
    row = lambda i: (i, 0)
    in_specs = [
        pl.BlockSpec((TOK_TILE, D_MODEL), row),
        pl.BlockSpec((1, 1, 6 * D_MODEL), lambda i: (i // tiles_per_mod, 0, 0)),
        pl.BlockSpec((1, D_MODEL), const),
        pl.BlockSpec((D_MODEL, C_END), const),
        pl.BlockSpec((1, ATTN_W), const),
        pl.BlockSpec((1, KV_W), const),
        pl.BlockSpec((1, LANES), const),
        pl.BlockSpec((1, LANES), const),
        pl.BlockSpec((GROUP_W, GROUP_W), const),
    ]
    args = [x2d, mod, n1, w_cat, qg, kg, alog, dtb, ones_bd]
    if rope:
        tiles_per_seq = seq_len // TOK_TILE
        in_specs += [pl.BlockSpec((TOK_TILE, LANES), lambda i: (i % tiles_per_seq, 0))] * 2
        args += list(rope_tabs)
    out_specs = [
        pl.BlockSpec((TOK_TILE, ATTN_W), row),
        pl.BlockSpec((TOK_TILE, 4 * LANES), row),
        pl.BlockSpec((TOK_TILE, 4 * DN_W), row),
        pl.BlockSpec((TOK_TILE, LANES), row),
        pl.BlockSpec((TOK_TILE, 2 * D_MODEL), row),
    ]
    out_shape = [
        jax.ShapeDtypeStruct((ntok, ATTN_W), BF16),
        jax.ShapeDtypeStruct((ntok, 4 * LANES), BF16),
        jax.ShapeDtypeStruct((ntok, 4 * DN_W), F32),
        jax.ShapeDtypeStruct((ntok, LANES), F32),
        jax.ShapeDtypeStruct((ntok, 2 * D_MODEL), F32),
    ]
    if cache_out:
        assert seq_len == TOK_TILE
        out_specs += [pl.BlockSpec((1, N_KV_HEADS, TOK_TILE, HEAD), lambda i: (i, 0, 0, 0))] * 2
        out_shape += [jax.ShapeDtypeStruct((ntiles, N_KV_HEADS, TOK_TILE, HEAD), F32)] * 2
    return pl.pallas_call(
        functools.partial(_inproj_kernel, rope=rope, cache_out=cache_out),
        grid=(ntiles,),
        in_specs=in_specs,
        out_specs=out_specs,
        out_shape=out_shape,
        compiler_params=pltpu.CompilerParams(dimension_semantics=("arbitrary",),
                                             vmem_limit_bytes=VMEM_LIMIT),
        name="inproj",
    )(*args)


def _attn_kernel(*refs, has_ctx):
    if has_ctx:
        q_ref, kv_ref, kvc_ref, o_ref = refs
    else:
        q_ref, kv_ref, o_ref = refs
    lane = lax.broadcasted_iota(jnp.int32, (1, LANES), 1)
    lower = lane < HEAD
    nt = (((1,), (1,)), ((), ()))
    blocks = []
    for j in range(ATTN_W // LANES):
        qb = q_ref[:, j * LANES:(j + 1) * LANES]
        kv_head = (2 * j) // (N_Q_HEADS // N_KV_HEADS)
        res = []
        for half in range(2):
            qh = jnp.where(lower if half == 0 else jnp.logical_not(lower), qb, jnp.zeros_like(qb))
            sel = 0 if half == kv_head else 1
            k_new = kv_ref[:, sel * LANES:(sel + 1) * LANES]
            v_new = kv_ref[:, (2 + sel) * LANES:(3 + sel) * LANES]
            s_n = lax.dot_general(qh, k_new, nt, preferred_element_type=F32)
            m = jnp.max(s_n, axis=-1, keepdims=True)
            if has_ctx:
                k_c = kvc_ref[:, sel * LANES:(sel + 1) * LANES]
                v_c = kvc_ref[:, (2 + sel) * LANES:(3 + sel) * LANES]
                s_c = lax.dot_general(qh, k_c, nt, preferred_element_type=F32)
                m = jnp.maximum(m, jnp.max(s_c, axis=-1, keepdims=True))
            p_n = jnp.exp(s_n - m)
            den = jnp.sum(p_n, axis=-1, keepdims=True)
            o = jnp.dot(p_n.astype(BF16), v_new, preferred_element_type=F32)
            if has_ctx:
                p_c = jnp.exp(s_c - m)
                den = den + jnp.sum(p_c, axis=-1, keepdims=True)
                o = o + jnp.dot(p_c.astype(BF16), v_c, preferred_element_type=F32)
            res.append(o / den)
        blocks.append(jnp.where(lower, res[0], res[1]))
    o_ref[...] = jnp.concatenate(blocks, axis=1).astype(BF16)


def _attention(q, kv, kvc, batch, seq_len):
    ntok = q.shape[0]
    nq = seq_len // TOK_TILE
    has_ctx = kvc is not None
    in_specs = [
        pl.BlockSpec((TOK_TILE, ATTN_W), lambda b, i: (b * nq + i, 0)),
        pl.BlockSpec((seq_len, 4 * LANES), lambda b, i: (b, 0)),
    ]
    args = [q, kv]
    if has_ctx:
        past = kvc.shape[0] // batch
        in_specs.append(pl.BlockSpec((past, 4 * LANES), lambda b, i: (b, 0)))
        args.append(kvc)
    return pl.pallas_call(
        functools.partial(_attn_kernel, has_ctx=has_ctx),
        grid=(batch, nq),
        in_specs=in_specs,
        out_specs=pl.BlockSpec((TOK_TILE, ATTN_W), lambda b, i: (b * nq + i, 0)),
        out_shape=jax.ShapeDtypeStruct((ntok, ATTN_W), BF16),
        compiler_params=pltpu.CompilerParams(dimension_semantics=("arbitrary", "arbitrary"),
                                             vmem_limit_bytes=VMEM_LIMIT),
        name="attention",
    )(*args)


def _block_diag(x):
    seg = lax.broadcasted_iota(jnp.int32, x.shape, 1) // HEAD
    return jnp.concatenate([jnp.where(seg == s, x, 0.0) for s in range(SEGS)], axis=0)


def _dn_kernel(*refs, fwd, nblk, has_state, want_state):
    x_ref, xp_ref, xn_ref, ab_ref, cw_ref, ones_ref, esel_ref = refs[:7]
    pos = 7
    if has_state:
        s0_ref = refs[pos]
        pos += 1
    o_ref = refs[pos]
    pos += 1
    if want_state:
        sf_ref = refs[pos]
        pos += 1
    s_scr = refs[pos]

    tb = TOK_TILE
    ngroups = DN_W // GROUP_W
    blk = pl.program_id(1)
    rb = blk if fwd else nblk - 1 - blk

    @pl.when(blk == 0)
    def _():
        if has_state:
            s_scr[...] = s0_ref[0]
        else:
            s_scr[...] = jnp.zeros_like(s_scr)

    x = x_ref[...]
    rows = lax.broadcasted_iota(jnp.int32, (tb, 1), 0)
    prev = jnp.where(rb > 0, xp_ref[7:8, :], 0.0)
    nxt = jnp.where(rb < nblk - 1, xn_ref[0:1, :], 0.0)
    xm1 = jnp.where(rows == 0, prev, pltpu.roll(x, 1, 0))
    xp1 = jnp.where(rows == tb - 1, nxt, pltpu.roll(x, tb - 1, 0))
    y = xm1 * cw_ref[0:1, :] + x * cw_ref[1:2, :] + xp1 * cw_ref[2:3, :]
    y = y * _sigmoid(y)

    ones_bd = ones_ref[...]
    q = y[:, 0:DN_W]
    k = y[:, DN_W:2 * DN_W]
    v = y[:, 2 * DN_W:3 * DN_W]
    q = q * lax.rsqrt(_seg_sumsq(q, ones_bd) + EPS) * (HEAD ** -0.5)
    k = k * lax.rsqrt(_seg_sumsq(k, ones_bd) + EPS)

    ex = _hdot(ab_ref[...], esel_ref[...])
    la = ex[:, 0:DN_W]
    be = ex[:, DN_W:2 * DN_W]
    same_chunk = ones_bd.astype(F32)
    r_i = lax.broadcasted_iota(jnp.int32, (tb, tb), 0) % CHUNK
    c_i = lax.broadcasted_iota(jnp.int32, (tb, tb), 1) % CHUNK
    tri = jnp.where((c_i <= r_i) if fwd else (c_i >= r_i), same_chunk, 0.0)
    g_col = _hdot(tri, la)
    t_i = lax.broadcasted_iota(jnp.int32, (tb, DN_W), 0) % CHUNK
    l_i = lax.broadcasted_iota(jnp.int32, (tb, DN_W), 1) % CHUNK
    g_row = _hdot(same_chunk, jnp.where((t_i <= l_i) if fwd else (t_i >= l_i), la, 0.0))

    ii = lax.broadcasted_iota(jnp.int32, (CHUNK, GROUP_W), 0)
    jj = lax.broadcasted_iota(jnp.int32, (CHUNK, GROUP_W), 1) % HEAD
    incl = (jj <= ii) if fwd else (jj >= ii)
    strict = (jj < ii) if fwd else (jj > ii)
    eye = jnp.where(jj == ii, 1.0, 0.0)
    nt = (((1,), (1,)), ((), ()))
    tn = (((0,), (0,)), ((), ()))
    nchunk = tb // CHUNK

    for step in range(nchunk):
        c = step if fwd else nchunk - 1 - step
        rs = slice(c * CHUNK, (c + 1) * CHUNK)
        for g in range(ngroups):
            ls = slice(g * GROUP_W, (g + 1) * GROUP_W)
            gc = g_col[rs, ls]
            bc = be[rs, ls]
            kc = k[rs, ls]
            qc = q[rs, ls]
            vc = v[rs, ls]
            dec = jnp.exp(jnp.where(incl, gc - g_row[rs, ls], -1e30))
            kq = lax.dot_general(jnp.concatenate([kc, qc], axis=0).astype(BF16),
                                 _block_diag(kc).astype(BF16), nt, preferred_element_type=F32)
            kk = kq[:CHUNK]
            qk = kq[CHUNK:]
            neg_l = -jnp.where(strict, bc * kk * dec, 0.0)
            acc = eye + neg_l
            power = neg_l
            for it in range(5):
                wmat = _block_diag(power)
                if it == 0:
                    power = _hdot(power, wmat)
                else:
                    r = _hdot(jnp.concatenate([acc, power], axis=0), wmat)
                    acc = acc + r[:CHUNK]
                    power = r[CHUNK:]
            acc = acc + _hdot(acc, _block_diag(power))
            eg_c = jnp.exp(gc)
            wu = _hdot(acc, jnp.concatenate([_block_diag(bc * eg_c * kc), _block_diag(bc * vc)],
                                            axis=1))
            w_c = wu[:, :GROUP_W]
            u_c = wu[:, GROUP_W:]
            qk_d = qk * dec
            q_dec = qc * eg_c
            g_last = gc[CHUNK - 1:CHUNK, :] if fwd else gc[0:1, :]
            k_dec = kc * jnp.exp(g_last - gc)
            eg_last = jnp.exp(g_last)

            s_bd = s_scr[g]
            r = _bdot(jnp.concatenate([w_c, q_dec], axis=0), s_bd)
            u_c = u_c - r[:CHUNK]
            o_c = r[CHUNK:] + _bdot(qk_d, _block_diag(u_c))
            upd = lax.dot_general(k_dec.astype(BF16), u_c.astype(BF16), tn,
                                  preferred_element_type=F32)
            s_scr[g] = s_bd * eg_last + same_chunk * upd
            o_ref[rs, ls] = o_c

    if want_state:
        @pl.when(blk == nblk - 1)
        def _():
            sf_ref[0] = s_scr[...]


def _deltanet(dcat, ab, conv_w, ones_bd, esel, s0_bd, batch, seq_len, fwd, want_state):
    ntok = dcat.shape[0]
    nblk = seq_len // TOK_TILE
    has_state = s0_bd is not None
    ngroups = DN_W // GROUP_W
    sub = TOK_TILE // 8

    def blkpos(b, i):
        return b * nblk + (i if fwd else nblk - 1 - i)

    in_specs = [
        pl.BlockSpec((TOK_TILE, 3 * DN_W), lambda b, i: (blkpos(b, i), 0)),
        pl.BlockSpec((8, 3 * DN_W), lambda b, i: (jnp.maximum(blkpos(b, i) * sub - 1, 0), 0)),
        pl.BlockSpec((8, 3 * DN_W),
                     lambda b, i: (jnp.minimum((blkpos(b, i) + 1) * sub, ntok // 8 - 1), 0)),
        pl.BlockSpec((TOK_TILE, LANES), lambda b, i: (blkpos(b, i), 0)),
        pl.BlockSpec((3, 3 * DN_W), lambda b, i: (0, 0)),
        pl.BlockSpec((GROUP_W, GROUP_W), lambda b, i: (0, 0)),
        pl.BlockSpec((LANES, 2 * DN_W), lambda b, i: (0, 0)),
    ]
    args = [dcat, dcat, dcat, ab, conv_w, ones_bd, esel]
    if has_state:
        in_specs.append(pl.BlockSpec((1, ngroups, GROUP_W, GROUP_W), lambda b, i: (b, 0, 0, 0)))
        args.append(s0_bd)
    out_specs = [pl.BlockSpec((TOK_TILE, DN_W), lambda b, i: (blkpos(b, i), 0))]
    out_shape = [jax.ShapeDtypeStruct((ntok, DN_W), F32)]
    if want_state:
        out_specs.append(pl.BlockSpec((1, ngroups, GROUP_W, GROUP_W), lambda b, i: (b, 0, 0, 0)))
        out_shape.append(jax.ShapeDtypeStruct((batch, ngroups, GROUP_W, GROUP_W), F32))
    return pl.pallas_call(
        functools.partial(_dn_kernel, fwd=fwd, nblk=nblk, has_state=has_state,
                          want_state=want_state),
        grid=(batch, nblk),
        in_specs=in_specs,
        out_specs=out_specs,
        out_shape=out_shape,
        scratch_shapes=[pltpu.VMEM((ngroups, GROUP_W, GROUP_W), F32)],
        compiler_params=pltpu.CompilerParams(dimension_semantics=("arbitrary", "arbitrary"),
                                             vmem_limit_bytes=VMEM_LIMIT),
        name="deltanet_fwd" if fwd else "deltanet_bwd",
    )(*args)


def _post_kernel(x_ref, oa_ref, of_ref, ob_ref, gout_ref, gates_ref, mod_ref, wpa_ref, wpd_ref,
                 wout_ref, dng_ref, ones_ref, o_ref):
    od = of_ref[...] + ob_ref[...]
    od = od * lax.rsqrt(_seg_sumsq(od, ones_ref[...]) * (1.0 / HEAD) + EPS) * dng_ref[...]
    gout = gout_ref[...]
    od = od * (gout * _sigmoid(gout))
    pa = jnp.dot(oa_ref[...], wpa_ref[...], preferred_element_type=F32)
    pd = jnp.dot(od.astype(BF16), wpd_ref[...], preferred_element_type=F32)
    merged = gates_ref[:, 0:D_MODEL] * pa + gates_ref[:, D_MODEL:2 * D_MODEL] * pd
    gate1 = mod_ref[0, :, 2 * D_MODEL:3 * D_MODEL]
    o_ref[...] = x_ref[...] + gate1 * jnp.dot(merged.astype(BF16), wout_ref[...],
                                              preferred_element_type=F32)


def _post(x2d, oa, o_f, o_b, dcat, gates, mod, tiles_per_mod, wpa, wpd, wout, dng, ones_bd):
    ntok = x2d.shape[0]
    row = lambda i: (i, 0)
    const = lambda i: (0, 0)
    return pl.pallas_call(
        _post_kernel,
        grid=(ntok // TOK_TILE,),
        in_specs=[
            pl.BlockSpec((TOK_TILE, D_MODEL), row),
            pl.BlockSpec((TOK_TILE, ATTN_W), row),
            pl.BlockSpec((TOK_TILE, DN_W), row),
            pl.BlockSpec((TOK_TILE, DN_W), row),
            pl.BlockSpec((TOK_TILE, DN_W), lambda i: (i, 3)),
            pl.BlockSpec((TOK_TILE, 2 * D_MODEL), row),
            pl.BlockSpec((1, 1, 6 * D_MODEL), lambda i: (i // tiles_per_mod, 0, 0)),
            pl.BlockSpec((ATTN_W, D_MODEL), const),
            pl.BlockSpec((DN_W, D_MODEL), const),
            pl.BlockSpec((D_MODEL, D_MODEL), const),
            pl.BlockSpec((1, DN_W), const),
            pl.BlockSpec((GROUP_W, GROUP_W), const),
        ],
        out_specs=pl.BlockSpec((TOK_TILE, D_MODEL), row),
        out_shape=jax.ShapeDtypeStruct((ntok, D_MODEL), F32),
        compiler_params=pltpu.CompilerParams(dimension_semantics=("arbitrary",),
                                             vmem_limit_bytes=VMEM_LIMIT),
        name="merge_outproj",
    )(x2d, oa, o_f, o_b, dcat, gates, mod, wpa, wpd, wout, dng, ones_bd)


def _mlp_kernel(*refs, final):
    x_ref, mod_ref, n2_ref, w1_ref, b1_ref, w2_ref, b2_ref = refs[:7]
    if final:
        fn_ref, o_ref = refs[7:9]
    else:
        o_ref = refs[7]
    x = x_ref[...]
    shift2 = mod_ref[0, :, 3 * D_MODEL:4 * D_MODEL]
    scale2 = mod_ref[0, :, 4 * D_MODEL:5 * D_MODEL]
    gate2 = mod_ref[0, :, 5 * D_MODEL:6 * D_MODEL]
    h = _rms_rows(x) * n2_ref[...]
    h = h * (1.0 + scale2) + shift2
    a = jnp.dot(h.astype(BF16), w1_ref[...], preferred_element_type=F32) + b1_ref[...]
    a = jnp.square(jnp.maximum(a, 0.0))
    ff = jnp.dot(a.astype(BF16), w2_ref[...], preferred_element_type=F32) + b2_ref[...]
    out = x + gate2 * ff
    if final:
        out = _rms_rows(out) * fn_ref[...]
    o_ref[...] = out


def _mlp(x2d, mod, tiles_per_mod, n2, w1, b1, w2, b2, final_norm):
    ntok = x2d.shape[0]
    final = final_norm is not None
    row = lambda i: (i, 0)
    const = lambda i: (0, 0)
    in_specs = [
        pl.BlockSpec((TOK_TILE, D_MODEL), row),
        pl.BlockSpec((1, 1, 6 * D_MODEL), lambda i: (i // tiles_per_mod, 0, 0)),
        pl.BlockSpec((1, D_MODEL), const),
        pl.BlockSpec((D_MODEL, D_FF), const),
        pl.BlockSpec((1, D_FF), const),
        pl.BlockSpec((D_FF, D_MODEL), const),
        pl.BlockSpec((1, D_MODEL), const),
    ]
    args = [x2d, mod, n2, w1, b1, w2, b2]
    if final:
        in_specs.append(pl.BlockSpec((1, D_MODEL), const))
        args.append(final_norm)
    return pl.pallas_call(
        functools.partial(_mlp_kernel, final=final),
        grid=(ntok // TOK_TILE,),
        in_specs=in_specs,
        out_specs=pl.BlockSpec((TOK_TILE, D_MODEL), row),
        out_shape=jax.ShapeDtypeStruct((ntok, D_MODEL), F32),
        compiler_params=pltpu.CompilerParams(dimension_semantics=("arbitrary",),
                                             vmem_limit_bytes=VMEM_LIMIT),
        name="mlp",
    )(*args)


def _rope_tables(n_tokens):
    freqs = HEAD // 4
    t = jnp.arange(n_tokens)
    row = (t // GRID_W).astype(F32)
    col = (t % GRID_W).astype(F32)
    inv = ROPE_THETA ** (-jnp.arange(freqs, dtype=F32) / freqs)
    ang = jnp.concatenate([row[:, None] * inv, col[:, None] * inv], axis=-1)
    cos, sin = jnp.cos(ang), jnp.sin(ang)
    cos_h = jnp.concatenate([cos, cos], axis=-1)
    sin_h = jnp.concatenate([-sin, sin], axis=-1)
    reps = LANES // HEAD
    return jnp.tile(cos_h, (1, reps)), jnp.tile(sin_h, (1, reps))


def _pack_w_in(w):
    pad = jnp.zeros((D_MODEL, LANES - 4 * N_DN_HEADS), w.dtype)
    return jnp.concatenate([w[:, :C_AB], w[:, C_AB:C_AB + 4 * N_DN_HEADS], pad,
                            w[:, C_AB + 4 * N_DN_HEADS:]], axis=1).astype(BF16)


def _pad_lanes(v):
    v = v.reshape(1, -1)
    return jnp.pad(v, ((0, 0), (0, LANES - v.shape[1])))


def _expand_selector(direction):
    src = jnp.arange(LANES)[:, None]
    dst = jnp.arange(2 * DN_W)[None, :]
    head = (dst % DN_W) // HEAD
    want = jnp.where(dst < DN_W, direction * N_DN_HEADS + head,
                     2 * N_DN_HEADS + direction * N_DN_HEADS + head)
    return (src == want).astype(F32)


def _kv_lane_orders(k, v):
    b, _, t, _ = k.shape
    tok = lambda a, order: jnp.concatenate([a[:, order[0]], a[:, order[1]]], axis=-1)
    out = jnp.concatenate([tok(k, (0, 1)), tok(k, (1, 0)), tok(v, (0, 1)), tok(v, (1, 0))], axis=-1)
    return out.reshape(b * t, 4 * LANES).astype(BF16)


def _state_to_block_diag(s):
    b = s.shape[0]
    s = s.reshape(b, DN_W // GROUP_W, SEGS, HEAD, HEAD).astype(F32)
    eye = jnp.eye(SEGS, dtype=F32)
    bd = s[:, :, :, :, None, :] * eye[None, None, :, None, :, None]
    return bd.reshape(b, DN_W // GROUP_W, GROUP_W, GROUP_W)


def _state_from_block_diag(bd):
    b = bd.shape[0]
    x = bd.reshape(b, DN_W // GROUP_W, SEGS, HEAD, SEGS, HEAD)
    s = jnp.stack([x[:, :, i, :, i, :] for i in range(SEGS)], axis=2)
    return s.reshape(b, N_DN_HEADS, HEAD, HEAD)


def kernel(x_prompt, x_sample, c, cache_k, cache_v, state_delta, c_ctx, w_mod, b_mod, norm1, norm2,
           w_in, conv_w, q_gain, k_gain, a_log, dt_bias, dn_gain, w_pa, w_pd, w_out, w1, b1, w2, b2,
           final_norm):
    depth = w_in.shape[0]
    batch, seq, _ = x_prompt.shape
    dec_batch, dec_seq, _ = x_sample.shape
    assert seq == TOK_TILE and dec_seq % TOK_TILE == 0

    mod_rows = 8
    cvec = jnp.concatenate([c, c_ctx[None, :],
                            jnp.zeros((mod_rows - dec_batch - 1, D_MODEL), F32)], axis=0)
    mod_all = _modulation(cvec, w_mod.astype(BF16), b_mod)

    ones_bd = (jnp.arange(GROUP_W)[:, None] // HEAD == jnp.arange(GROUP_W)[None, :] // HEAD
               ).astype(BF16)
    esel = [_expand_selector(0), _expand_selector(1)]
    rope_tabs = _rope_tables(dec_seq)

    xp = x_prompt.reshape(batch * seq, D_MODEL)
    xs = x_sample.reshape(dec_batch * dec_seq, D_MODEL)
    ks_out, vs_out, st_out = [], [], []
    for l in range(depth):
        w_cat = _pack_w_in(w_in[l])
        n1 = norm1[l].reshape(1, D_MODEL)
        n2 = norm2[l].reshape(1, D_MODEL)
        qg = jnp.tile(q_gain[l], N_Q_HEADS).reshape(1, ATTN_W)
        kg = jnp.tile(k_gain[l], N_KV_HEADS).reshape(1, KV_W)
        dng = jnp.tile(dn_gain[l], N_DN_HEADS).reshape(1, DN_W)
        alog = _pad_lanes(a_log[l])
        dtb = _pad_lanes(dt_bias[l])
        wpa, wpd, wout = w_pa[l].astype(BF16), w_pd[l].astype(BF16), w_out[l].astype(BF16)
        w1b, w2b = w1[l].astype(BF16), w2[l].astype(BF16)
        b1r, b2r = b1[l].reshape(1, D_FF), b2[l].reshape(1, D_MODEL)
        last = l == depth - 1
        fin = final_norm.reshape(1, D_MODEL) if last else None
        mod_lat = mod_all[l, :dec_batch].reshape(dec_batch, 1, 6 * D_MODEL)
        mod_ctx = mod_all[l, dec_batch:dec_batch + 1].reshape(1, 1, 6 * D_MODEL)

        tpm = batch * seq // TOK_TILE
        q, kv, dcat, ab, gates, k_c, v_c = _inproj(xp, mod_ctx, tpm, n1, w_cat, qg, kg, alog, dtb,
                                                   ones_bd, None, True, seq)
        oa = _attention(q, kv, None, batch, seq)
        o_f, s_f = _deltanet(dcat, ab, conv_w[l], ones_bd, esel[0], None, batch, seq, True, True)
        o_b, s_b = _deltanet(dcat, ab, conv_w[l], ones_bd, esel[1], None, batch, seq, False, True)
        x1 = _post(xp, oa, o_f, o_b, dcat, gates, mod_ctx, tpm, wpa, wpd, wout, dng, ones_bd)
        xp = _mlp(x1, mod_ctx, tpm, n2, w1b, b1r, w2b, b2r, fin)
        ks_out.append(k_c)
        vs_out.append(v_c)
        st_out.append(jnp.stack([_state_from_block_diag(s_f), _state_from_block_diag(s_b)], axis=1))

        tpm = dec_seq // TOK_TILE
        kvc = _kv_lane_orders(cache_k[:, l], cache_v[:, l])
        q, kv, dcat, ab, gates = _inproj(xs, mod_lat, tpm, n1, w_cat, qg, kg, alog, dtb, ones_bd,
                                         rope_tabs, False, dec_seq)
        oa = _attention(q, kv, kvc, dec_batch, dec_seq)
        s0f = _state_to_block_diag(state_delta[:, l, 0])
        s0b = _state_to_block_diag(state_delta[:, l, 1])
        (o_f,) = _deltanet(dcat, ab, conv_w[l], ones_bd, esel[0], s0f, dec_batch, dec_seq, True, False)
        (o_b,) = _deltanet(dcat, ab, conv_w[l], ones_bd, esel[1], s0b, dec_batch, dec_seq, False, False)
        x1 = _post(xs, oa, o_f, o_b, dcat, gates, mod_lat, tpm, wpa, wpd, wout, dng, ones_bd)
        xs = _mlp(x1, mod_lat, tpm, n2, w1b, b1r, w2b, b2r, fin)

    y_prompt = xp.reshape(batch, seq, D_MODEL)
    y_sample = xs.reshape(dec_batch, dec_seq, D_MODEL)
    return (y_prompt, y_sample, jnp.stack(ks_out, axis=1), jnp.stack(vs_out, axis=1),
            jnp.stack(st_out, axis=1))
```

```python
import functools

import jax
import jax.numpy as jnp
from jax import lax
from jax.experimental import pallas as pl
from jax.experimental.pallas import tpu as pltpu

F32 = jnp.float32
BF16 = jnp.bfloat16

D_MODEL = 1024
HEAD = 64
N_Q_HEADS = 8
N_KV_HEADS = 2
N_DN_HEADS = 8
ATTN_W = N_Q_HEADS * HEAD
KV_W = N_KV_HEADS * HEAD
DN_W = N_DN_HEADS * HEAD
CHUNK = 64
GRID_W = 64
D_FF = 4 * D_MODEL
ROPE_THETA = 10000.0
EPS = 1e-6

LANES = 128
GROUP_W = 256
SEGS = GROUP_W // HEAD
TOK_TILE = 256
VMEM_LIMIT = 56 * 1024 * 1024

C_A = 0
C_D = ATTN_W + 2 * KV_W
C_AB = C_D + 4 * DN_W
C_G = C_AB + LANES
C_END = C_G + 2 * D_MODEL


def _sigmoid(x):
    return 1.0 / (1.0 + jnp.exp(-x))


def _bdot(a, b):
    return jnp.dot(a.astype(BF16), b.astype(BF16), preferred_element_type=F32)


def _split2(x):
    hi = x.astype(BF16)
    return hi, (x - hi.astype(F32)).astype(BF16)


def _split3(x):
    hi = x.astype(BF16)
    r = x - hi.astype(F32)
    mid = r.astype(BF16)
    return hi, mid, (r - mid.astype(F32)).astype(BF16)


def _sel_dot_left(x, sel3):
    return jnp.dot(jnp.concatenate(_split3(x), axis=1), sel3, preferred_element_type=F32)


def _sel_dot_right(sel3, x):
    return jnp.dot(sel3, jnp.concatenate(_split3(x), axis=0), preferred_element_type=F32)


def _seg_sumsq(x, ones_bd):
    x2 = x * x
    hi = x2.astype(BF16)
    lo = (x2 - hi.astype(F32)).astype(BF16)
    width = x.shape[1]
    if width == LANES:
        m = ones_bd[:LANES, :LANES]
        return (jnp.dot(hi, m, preferred_element_type=F32)
                + jnp.dot(lo, m, preferred_element_type=F32))
    outs = []
    for g in range(width // GROUP_W):
        sl = slice(g * GROUP_W, (g + 1) * GROUP_W)
        outs.append(jnp.dot(hi[:, sl], ones_bd, preferred_element_type=F32)
                    + jnp.dot(lo[:, sl], ones_bd, preferred_element_type=F32))
    return jnp.concatenate(outs, axis=1)


def _rms_rows(x):
    return x * lax.rsqrt(jnp.mean(x * x, axis=-1, keepdims=True) + EPS)


def _mod_kernel(c_ref, w_ref, b_ref, o_ref):
    c = c_ref[...]
    s = c * _sigmoid(c)
    o_ref[0] = jnp.dot(s.astype(BF16), w_ref[0], preferred_element_type=F32) + b_ref[0]


def _modulation(cvec, w_mod_bf, b_mod):
    depth = w_mod_bf.shape[0]
    rows = cvec.shape[0]
    nblk = 6 * D_MODEL // D_MODEL
    return pl.pallas_call(
        _mod_kernel,
        grid=(depth, nblk),
        in_specs=[
            pl.BlockSpec((rows, D_MODEL), lambda l, j: (0, 0)),
            pl.BlockSpec((1, D_MODEL, D_MODEL), lambda l, j: (l, 0, j)),
            pl.BlockSpec((1, 1, D_MODEL), lambda l, j: (l, 0, j)),
        ],
        out_specs=pl.BlockSpec((1, rows, D_MODEL), lambda l, j: (l, 0, j)),
        out_shape=jax.ShapeDtypeStruct((depth, rows, 6 * D_MODEL), F32),
        name="adaln_mod",
    )(cvec, w_mod_bf, b_mod.reshape(depth, 1, 6 * D_MODEL))


def _rope(x, cos, sin_signed):
    width = x.shape[1]
    lane = lax.broadcasted_iota(jnp.int32, x.shape, 1)
    first_half = (lane % HEAD) < (HEAD // 2)
    rot = jnp.where(first_half, pltpu.roll(x, width - HEAD // 2, 1), pltpu.roll(x, HEAD // 2, 1))
    return x * cos + rot * sin_signed


def _inproj_kernel(*refs, rope, cache_out):
    (x_ref, mod_ref, n1_ref, w_ref, wvt_ref, qg_ref, kg_ref, alog_ref, dtb_ref,
     ones_ref) = refs[:10]
    pos = 10
    if rope:
        cos_ref, sin_ref = refs[pos:pos + 2]
        pos += 2
    q_ref, k2_ref, vt_ref, d_ref, ab_ref, g_ref = refs[pos:pos + 6]
    pos += 6
    if cache_out:
        kc_ref, vc_ref = refs[pos:pos + 2]

    shift1 = mod_ref[0, :, 0:D_MODEL]
    scale1 = mod_ref[0, :, D_MODEL:2 * D_MODEL]
    h = _rms_rows(x_ref[...]) * n1_ref[...]
    h = h * (1.0 + scale1) + shift1
    h16 = h.astype(BF16)
    y = jnp.dot(h16, w_ref[...], preferred_element_type=F32)
    ones_bd = ones_ref[...]

    vt = lax.dot_general(wvt_ref[...], h16, (((1,), (1,)), ((), ())), preferred_element_type=F32)
    vt_ref[0, 0:KV_W, :] = vt.astype(BF16)
    vt_ref[0, KV_W:2 * KV_W, :] = jnp.concatenate([vt[HEAD:], vt[:HEAD]], axis=0).astype(BF16)

    qa = y[:, C_A:C_A + ATTN_W]
    qn = qa * lax.rsqrt(_seg_sumsq(qa, ones_bd) * (1.0 / HEAD) + EPS) * qg_ref[...]
    ka = y[:, C_A + ATTN_W:C_A + ATTN_W + KV_W]
    kn = ka * lax.rsqrt(_seg_sumsq(ka, ones_bd) * (1.0 / HEAD) + EPS) * kg_ref[...]
    va = y[:, C_A + ATTN_W + KV_W:C_D]
    if cache_out:
        kc_ref[0, 0] = kn[:, :HEAD]
        kc_ref[0, 1] = kn[:, HEAD:]
        vc_ref[0, 0] = va[:, :HEAD]
        vc_ref[0, 1] = va[:, HEAD:]
    if rope:
        cos = cos_ref[...]
        sin = sin_ref[...]
        qn = _rope(qn, jnp.concatenate([cos] * (ATTN_W // LANES), axis=1),
                   jnp.concatenate([sin] * (ATTN_W // LANES), axis=1))
        kn = _rope(kn, cos, sin)
    q_ref[...] = (qn * (HEAD ** -0.5)).astype(BF16)
    k2_ref[0, :, 0:LANES] = kn.astype(BF16)
    k2_ref[0, :, LANES:2 * LANES] = pltpu.roll(kn, HEAD, 1).astype(BF16)

    d_ref[...] = y[:, C_D:C_AB]

    z = y[:, C_AB:C_G]
    zb = z + dtb_ref[...]
    softplus = jnp.maximum(zb, 0.0) + jnp.log(1.0 + jnp.exp(-jnp.abs(zb)))
    log_a = -jnp.exp(alog_ref[...]) * softplus
    lane = lax.broadcasted_iota(jnp.int32, z.shape, 1)
    ab_ref[...] = jnp.where(lane < 2 * N_DN_HEADS, log_a,
                            jnp.where(lane < 4 * N_DN_HEADS, _sigmoid(z), 0.0))

    g_ref[...] = _sigmoid(y[:, C_G:C_END])


def _inproj(x2d, mod, tiles_per_mod, n1, w_cat, wvt, qg, kg, alog, dtb, ones_bd, rope_tabs,
            cache_out, seq_len):
    ntok = x2d.shape[0]
    ntiles = ntok // TOK_TILE
    rope = rope_tabs is not None
    const = lambda i: (0, 0)
    row = lambda i: (i, 0)
    in_specs = [
        pl.BlockSpec((TOK_TILE, D_MODEL), row),
        pl.BlockSpec((1, 1, 6 * D_MODEL), lambda i: (i // tiles_per_mod, 0, 0)),
        pl.BlockSpec((1, D_MODEL), const),
        pl.BlockSpec((D_MODEL, C_END), const),
        pl.BlockSpec((KV_W, D_MODEL), const),
        pl.BlockSpec((1, ATTN_W), const),
        pl.BlockSpec((1, KV_W), const),
        pl.BlockSpec((1, LANES), const),
        pl.BlockSpec((1, LANES), const),
        pl.BlockSpec((GROUP_W, GROUP_W), const),
    ]
    args = [x2d, mod, n1, w_cat, wvt, qg, kg, alog, dtb, ones_bd]
    if rope:
        tiles_per_seq = seq_len // TOK_TILE
        in_specs += [pl.BlockSpec((TOK_TILE, LANES), lambda i: (i % tiles_per_seq, 0))] * 2
        args += list(rope_tabs)
    out_specs = [
        pl.BlockSpec((TOK_TILE, ATTN_W), row),
        pl.BlockSpec((1, TOK_TILE, 2 * LANES), lambda i: (i, 0, 0)),
        pl.BlockSpec((1, 2 * KV_W, TOK_TILE), lambda i: (i, 0, 0)),
        pl.BlockSpec((TOK_TILE, 4 * DN_W), row),
        pl.BlockSpec((TOK_TILE, LANES), row),
        pl.BlockSpec((TOK_TILE, 2 * D_MODEL), row),
    ]
    out_shape = [
        jax.ShapeDtypeStruct((ntok, ATTN_W), BF16),
        jax.ShapeDtypeStruct((ntiles, TOK_TILE, 2 * LANES), BF16),
        jax.ShapeDtypeStruct((ntiles, 2 * KV_W, TOK_TILE), BF16),
        jax.ShapeDtypeStruct((ntok, 4 * DN_W), F32),
        jax.ShapeDtypeStruct((ntok, LANES), F32),
        jax.ShapeDtypeStruct((ntok, 2 * D_MODEL), F32),
    ]
    if cache_out:
        assert seq_len == TOK_TILE
        out_specs += [pl.BlockSpec((1, N_KV_HEADS, TOK_TILE, HEAD), lambda i: (i, 0, 0, 0))] * 2
        out_shape += [jax.ShapeDtypeStruct((ntiles, N_KV_HEADS, TOK_TILE, HEAD), F32)] * 2
    return pl.pallas_call(
        functools.partial(_inproj_kernel, rope=rope, cache_out=cache_out),
        grid=(ntiles,),
        in_specs=in_specs,
        out_specs=out_specs,
        out_shape=out_shape,
        compiler_params=pltpu.CompilerParams(dimension_semantics=("arbitrary",),
                                             vmem_limit_bytes=VMEM_LIMIT),
        name="inproj",
    )(*args)


def _attn_kernel(*refs, has_ctx):
    if has_ctx:
        q_ref, k2_ref, vt_ref, k2c_ref, vtc_ref, o_ref = refs
    else:
        q_ref, k2_ref, vt_ref, o_ref = refs
    nchunks = k2_ref.shape[0]
    tq = q_ref.shape[0]
    lane = lax.broadcasted_iota(jnp.int32, (1, LANES), 1)
    lower = lane < HEAD
    nt = (((1,), (1,)), ((), ()))
    qh, sel = [], []
    for head in range(N_Q_HEADS):
        j, half = divmod(head, 2)
        qb = q_ref[:, j * LANES:(j + 1) * LANES]
        qh.append(jnp.where(lower if half == 0 else jnp.logical_not(lower), qb, jnp.zeros_like(qb)))
        sel.append(0 if half == head // (N_Q_HEADS // N_KV_HEADS) else 1)

    def update(carry, k_chunk, vt_chunk):
        s = [lax.dot_general(k_chunk[:, sl * LANES:(sl + 1) * LANES], qx, nt,
                             preferred_element_type=F32) for sl, qx in zip(sel, qh)]
        m_new = [jnp.maximum(m, jnp.max(sx, axis=0, keepdims=True)) for (m, _, _), sx in zip(carry, s)]
        alpha = [jnp.exp(m - mn) for (m, _, _), mn in zip(carry, m_new)]
        p = [jnp.exp(sx - mn) for sx, mn in zip(s, m_new)]
        den = [a * d + jnp.sum(px, axis=0, keepdims=True) for a, (_, d, _), px in zip(alpha, carry, p)]
        pv = [jnp.dot(vt_chunk[sl * KV_W:(sl + 1) * KV_W, :], px.astype(BF16),
                      preferred_element_type=F32) for sl, px in zip(sel, p)]
        acc = [a * ac + x for a, (_, _, ac), x in zip(alpha, carry, pv)]
        return tuple(zip(m_new, den, acc))

    carry = tuple((jnp.full((1, tq), -1e30, F32), jnp.zeros((1, tq), F32),
                   jnp.zeros((KV_W, tq), F32)) for _ in range(N_Q_HEADS))
    if has_ctx:
        carry = update(carry, k2c_ref[0], vtc_ref[0])
    carry = lax.fori_loop(0, nchunks, lambda c, cr: update(cr, k2_ref[c], vt_ref[c]), carry,
                          unroll=2 if nchunks % 2 == 0 else 1)
    res = [(acc / den).T for _, den, acc in carry]
    blocks = [jnp.where(lower, res[2 * j], res[2 * j + 1]) for j in range(ATTN_W // LANES)]
    o_ref[...] = jnp.concatenate(blocks, axis=1).astype(BF16)


def _attention(q, k2, vt, k2c, vtc, batch, seq_len):
    ntok = q.shape[0]
    nq = seq_len // TOK_TILE
    has_ctx = k2c is not None
    chunk_block = (nq, TOK_TILE, 2 * LANES)
    in_specs = [
        pl.BlockSpec((TOK_TILE, ATTN_W), lambda b, i: (b * nq + i, 0)),
        pl.BlockSpec(chunk_block, lambda b, i: (b, 0, 0)),
        pl.BlockSpec(chunk_block, lambda b, i: (b, 0, 0)),
    ]
    args = [q, k2, vt]
    if has_ctx:
        in_specs += [pl.BlockSpec((1,) + k2c.shape[1:], lambda b, i: (b, 0, 0)),
                     pl.BlockSpec((1,) + vtc.shape[1:], lambda b, i: (b, 0, 0))]
        args += [k2c, vtc]
    return pl.pallas_call(
        functools.partial(_attn_kernel, has_ctx=has_ctx),
        grid=(batch, nq),
        in_specs=in_specs,
        out_specs=pl.BlockSpec((TOK_TILE, ATTN_W), lambda b, i: (b * nq + i, 0)),
        out_shape=jax.ShapeDtypeStruct((ntok, ATTN_W), BF16),
        compiler_params=pltpu.CompilerParams(dimension_semantics=("arbitrary", "arbitrary"),
                                             vmem_limit_bytes=VMEM_LIMIT),
        name="attention",
    )(*args)


def _block_diag(x):
    seg = lax.broadcasted_iota(jnp.int32, x.shape, 1) // HEAD
    return jnp.concatenate([jnp.where(seg == s, x, jnp.zeros_like(x)) for s in range(SEGS)], axis=0)


def _dn_kernel(*refs, fwd, nblk, has_state, want_state):
    x_ref, xp_ref, xn_ref, ab_ref, cw_ref, ones_ref, esel_ref = refs[:7]
    pos = 7
    if has_state:
        s0_ref = refs[pos]
        pos += 1
    o_ref = refs[pos]
    pos += 1
    if want_state:
        sf_ref = refs[pos]
        pos += 1
    s_scr = refs[pos]

    tb = TOK_TILE
    ngroups = DN_W // GROUP_W
    blk = pl.program_id(1)
    rb = blk if fwd else nblk - 1 - blk

    @pl.when(blk == 0)
    def _():
        if has_state:
            s_scr[...] = s0_ref[0]
        else:
            s_scr[...] = jnp.zeros_like(s_scr)

    x = x_ref[...]
    rows = lax.broadcasted_iota(jnp.int32, (tb, 1), 0)
    prev = jnp.where(rb > 0, xp_ref[7:8, :], 0.0)
    nxt = jnp.where(rb < nblk - 1, xn_ref[0:1, :], 0.0)
    xm1 = jnp.where(rows == 0, prev, pltpu.roll(x, 1, 0))
    xp1 = jnp.where(rows == tb - 1, nxt, pltpu.roll(x, tb - 1, 0))
    y = xm1 * cw_ref[0:1, :] + x * cw_ref[1:2, :] + xp1 * cw_ref[2:3, :]
    y = y * _sigmoid(y)

    ones_bd = ones_ref[...]
    q = y[:, 0:DN_W]
    k = y[:, DN_W:2 * DN_W]
    v = y[:, 2 * DN_W:3 * DN_W]
    q = q * lax.rsqrt(_seg_sumsq(q, ones_bd) + EPS) * (HEAD ** -0.5)
    k = k * lax.rsqrt(_seg_sumsq(k, ones_bd) + EPS)

    ex = _sel_dot_left(ab_ref[...], esel_ref[...])
    la = ex[:, 0:DN_W]
    be = ex[:, DN_W:2 * DN_W]
    same_chunk = ones_bd.astype(F32)
    r_i = lax.broadcasted_iota(jnp.int32, (tb, tb), 0) % CHUNK
    c_i = lax.broadcasted_iota(jnp.int32, (tb, tb), 1) % CHUNK
    tri = jnp.where((c_i <= r_i) if fwd else (c_i >= r_i), ones_bd, jnp.zeros_like(ones_bd))
    g_col = _sel_dot_right(jnp.concatenate([tri] * 3, axis=1), la)
    t_i = lax.broadcasted_iota(jnp.int32, (tb, DN_W), 0) % CHUNK
    l_i = lax.broadcasted_iota(jnp.int32, (tb, DN_W), 1) % CHUNK
    g_row = _sel_dot_right(jnp.concatenate([ones_bd] * 3, axis=1),
                           jnp.where((t_i <= l_i) if fwd else (t_i >= l_i), la, 0.0))

    ii = lax.broadcasted_iota(jnp.int32, (CHUNK, GROUP_W), 0)
    jj = lax.broadcasted_iota(jnp.int32, (CHUNK, GROUP_W), 1) % HEAD
    incl = (jj <= ii) if fwd else (jj >= ii)
    strict = (jj < ii) if fwd else (jj > ii)
    eye = jnp.where(jj == ii, 1.0, 0.0)
    nt = (((1,), (1,)), ((), ()))
    tn = (((0,), (0,)), ((), ()))
    nchunk = tb // CHUNK

    order = [c if fwd else nchunk - 1 - c for c in range(nchunk)]
    pairs = [(slice(c * CHUNK, (c + 1) * CHUNK), slice(g * GROUP_W, (g + 1) * GROUP_W), g)
             for c in order for g in range(ngroups)]
    dot = functools.partial(jnp.dot, preferred_element_type=F32)

    dec, kq = [], []
    for rs, ls, _ in pairs:
        dec.append(jnp.exp(jnp.where(incl, g_col[rs, ls] - g_row[rs, ls], -1e30)))
        kc16 = k[rs, ls].astype(BF16)
        kq.append(lax.dot_general(jnp.concatenate([kc16, q[rs, ls].astype(BF16)], axis=0),
                                  _block_diag(kc16), nt, preferred_element_type=F32))

    neg_l = [-jnp.where(strict, be[rs, ls] * kqp[:CHUNK] * d, 0.0)
             for (rs, ls, _), kqp, d in zip(pairs, kq, dec)]
    acc = [eye + n for n in neg_l]
    power = list(neg_l)
    for it in range(5):
        for p in range(len(pairs)):
            wmat = _block_diag(power[p].astype(BF16))
            if it == 0:
                power[p] = dot(power[p].astype(BF16), wmat)
            else:
                r = dot(jnp.concatenate([acc[p], power[p]], axis=0).astype(BF16), wmat)
                acc[p] = acc[p] + r[:CHUNK]
                power[p] = r[CHUNK:]
    acc = [a + dot(a.astype(BF16), _block_diag(pw.astype(BF16))) for a, pw in zip(acc, power)]

    xs = [_split2(a) for a in acc]
    ns = [_split2(n) for n in neg_l]
    nx1 = [dot(jnp.concatenate([nh, nl], axis=0), _block_diag(xh))
           for (nh, nl), (xh, _) in zip(ns, xs)]
    nx2 = [dot(nh, _block_diag(xl)) for (nh, _), (_, xl) in zip(ns, xs)]
    resid = [eye - a + (n1[:CHUNK] + n1[CHUNK:] + n2) for a, n1, n2 in zip(acc, nx1, nx2)]
    acc = [a + dot(xh, _block_diag(rsd.astype(BF16))) for a, (xh, _), rsd in zip(acc, xs, resid)]

    eg_c = [jnp.exp(g_col[rs, ls]) for rs, ls, _ in pairs]
    wu = []
    for (rs, ls, _), a, eg in zip(pairs, acc, eg_c):
        rw = (be[rs, ls] * eg * k[rs, ls]).astype(BF16)
        ru = (be[rs, ls] * v[rs, ls]).astype(BF16)
        wu.append(dot(a.astype(BF16), jnp.concatenate([_block_diag(rw), _block_diag(ru)], axis=1)))

    lhs, qk_d, k_dec, eg_last = [], [], [], []
    for (rs, ls, _), kqp, d, eg, wup in zip(pairs, kq, dec, eg_c, wu):
        gc = g_col[rs, ls]
        g_last = gc[CHUNK - 1:CHUNK, :] if fwd else gc[0:1, :]
        lhs.append(jnp.concatenate([wup[:, :GROUP_W], q[rs, ls] * eg], axis=0).astype(BF16))
        qk_d.append((kqp[CHUNK:] * d).astype(BF16))
        k_dec.append((k[rs, ls] * jnp.exp(g_last - gc)).astype(BF16))
        eg_last.append(jnp.exp(g_last))

    for step in range(nchunk):
        ps = range(step * ngroups, (step + 1) * ngroups)
        s_bd = [s_scr[pairs[p][2]] for p in ps]
        r = [dot(lhs[p], s.astype(BF16)) for p, s in zip(ps, s_bd)]
        u16 = [(wu[p][:, GROUP_W:] - rp[:CHUNK]).astype(BF16) for p, rp in zip(ps, r)]
        o_c = [rp[CHUNK:] + dot(qk_d[p], _block_diag(u)) for p, rp, u in zip(ps, r, u16)]
        upd = [lax.dot_general(k_dec[p], u, tn, preferred_element_type=F32) for p, u in zip(ps, u16)]
        for p, s, o, up in zip(ps, s_bd, o_c, upd):
            rs, ls, g = pairs[p]
            s_scr[g] = s * eg_last[p] + same_chunk * up
            o_ref[rs, ls] = o

    if want_state:
        @pl.when(blk == nblk - 1)
        def _():
            sf_ref[0] = s_scr[...]


def _deltanet(dcat, ab, conv_w, ones_bd, esel, s0_bd, batch, seq_len, fwd, want_state):
    ntok = dcat.shape[0]
    nblk = seq_len // TOK_TILE
    has_state = s0_bd is not None
    ngroups = DN_W // GROUP_W
    sub = TOK_TILE // 8

    def blkpos(b, i):
        return b * nblk + (i if fwd else nblk - 1 - i)

    in_specs = [
        pl.BlockSpec((TOK_TILE, 3 * DN_W), lambda b, i: (blkpos(b, i), 0)),
        pl.BlockSpec((8, 3 * DN_W), lambda b, i: (jnp.maximum(blkpos(b, i) * sub - 1, 0), 0)),
        pl.BlockSpec((8, 3 * DN_W),
                     lambda b, i: (jnp.minimum((blkpos(b, i) + 1) * sub, ntok // 8 - 1), 0)),
        pl.BlockSpec((TOK_TILE, LANES), lambda b, i: (blkpos(b, i), 0)),
        pl.BlockSpec((3, 3 * DN_W), lambda b, i: (0, 0)),
        pl.BlockSpec((GROUP_W, GROUP_W), lambda b, i: (0, 0)),
        pl.BlockSpec((3 * LANES, 2 * DN_W), lambda b, i: (0, 0)),
    ]
    args = [dcat, dcat, dcat, ab, conv_w, ones_bd, esel]
    if has_state:
        in_specs.append(pl.BlockSpec((1, ngroups, GROUP_W, GROUP_W), lambda b, i: (b, 0, 0, 0)))
        args.append(s0_bd)
    out_specs = [pl.BlockSpec((TOK_TILE, DN_W), lambda b, i: (blkpos(b, i), 0))]
    out_shape = [jax.ShapeDtypeStruct((ntok, DN_W), F32)]
    if want_state:
        out_specs.append(pl.BlockSpec((1, ngroups, GROUP_W, GROUP_W), lambda b, i: (b, 0, 0, 0)))
        out_shape.append(jax.ShapeDtypeStruct((batch, ngroups, GROUP_W, GROUP_W), F32))
    return pl.pallas_call(
        functools.partial(_dn_kernel, fwd=fwd, nblk=nblk, has_state=has_state,
                          want_state=want_state),
        grid=(batch, nblk),
        in_specs=in_specs,
        out_specs=out_specs,
        out_shape=out_shape,
        scratch_shapes=[pltpu.VMEM((ngroups, GROUP_W, GROUP_W), F32)],
        compiler_params=pltpu.CompilerParams(dimension_semantics=("arbitrary", "arbitrary"),
                                             vmem_limit_bytes=VMEM_LIMIT),
        name="deltanet_fwd" if fwd else "deltanet_bwd",
    )(*args)


def _post_kernel(x_ref, oa_ref, of_ref, ob_ref, gout_ref, gates_ref, mod_ref, wpa_ref, wpd_ref,
                 wout_ref, dng_ref, ones_ref, o_ref):
    od = of_ref[...] + ob_ref[...]
    od = od * lax.rsqrt(_seg_sumsq(od, ones_ref[...]) * (1.0 / HEAD) + EPS) * dng_ref[...]
    gout = gout_ref[...]
    od = od * (gout * _sigmoid(gout))
    pa = jnp.dot(oa_ref[...], wpa_ref[...], preferred_element_type=F32)
    pd = jnp.dot(od.astype(BF16), wpd_ref[...], preferred_element_type=F32)
    merged = gates_ref[:, 0:D_MODEL] * pa + gates_ref[:, D_MODEL:2 * D_MODEL] * pd
    gate1 = mod_ref[0, :, 2 * D_MODEL:3 * D_MODEL]
    o_ref[...] = x_ref[...] + gate1 * jnp.dot(merged.astype(BF16), wout_ref[...],
                                              preferred_element_type=F32)


def _post(x2d, oa, o_f, o_b, dcat, gates, mod, tiles_per_mod, wpa, wpd, wout, dng, ones_bd):
    ntok = x2d.shape[0]
    row = lambda i: (i, 0)
    const = lambda i: (0, 0)
    return pl.pallas_call(
        _post_kernel,
        grid=(ntok // TOK_TILE,),
        in_specs=[
            pl.BlockSpec((TOK_TILE, D_MODEL), row),
            pl.BlockSpec((TOK_TILE, ATTN_W), row),
            pl.BlockSpec((TOK_TILE, DN_W), row),
            pl.BlockSpec((TOK_TILE, DN_W), row),
            pl.BlockSpec((TOK_TILE, DN_W), lambda i: (i, 3)),
            pl.BlockSpec((TOK_TILE, 2 * D_MODEL), row),
            pl.BlockSpec((1, 1, 6 * D_MODEL), lambda i: (i // tiles_per_mod, 0, 0)),
            pl.BlockSpec((ATTN_W, D_MODEL), const),
            pl.BlockSpec((DN_W, D_MODEL), const),
            pl.BlockSpec((D_MODEL, D_MODEL), const),
            pl.BlockSpec((1, DN_W), const),
            pl.BlockSpec((GROUP_W, GROUP_W), const),
        ],
        out_specs=pl.BlockSpec((TOK_TILE, D_MODEL), row),
        out_shape=jax.ShapeDtypeStruct((ntok, D_MODEL), F32),
        compiler_params=pltpu.CompilerParams(dimension_semantics=("arbitrary",),
                                             vmem_limit_bytes=VMEM_LIMIT),
        name="merge_outproj",
    )(x2d, oa, o_f, o_b, dcat, gates, mod, wpa, wpd, wout, dng, ones_bd)


def _mlp_kernel(*refs, final):
    x_ref, mod_ref, n2_ref, w1_ref, b1_ref, w2_ref, b2_ref = refs[:7]
    if final:
        fn_ref, o_ref = refs[7:9]
    else:
        o_ref = refs[7]
    x = x_ref[...]
    shift2 = mod_ref[0, :, 3 * D_MODEL:4 * D_MODEL]
    scale2 = mod_ref[0, :, 4 * D_MODEL:5 * D_MODEL]
    gate2 = mod_ref[0, :, 5 * D_MODEL:6 * D_MODEL]
    h = _rms_rows(x) * n2_ref[...]
    h = h * (1.0 + scale2) + shift2
    a = jnp.dot(h.astype(BF16), w1_ref[...], preferred_element_type=F32) + b1_ref[...]
    a = jnp.square(jnp.maximum(a, 0.0))
    ff = jnp.dot(a.astype(BF16), w2_ref[...], preferred_element_type=F32) + b2_ref[...]
    out = x + gate2 * ff
    if final:
        out = _rms_rows(out) * fn_ref[...]
    o_ref[...] = out


def _mlp(x2d, mod, tiles_per_mod, n2, w1, b1, w2, b2, final_norm):
    ntok = x2d.shape[0]
    final = final_norm is not None
    row = lambda i: (i, 0)
    const = lambda i: (0, 0)
    in_specs = [
        pl.BlockSpec((TOK_TILE, D_MODEL), row),
        pl.BlockSpec((1, 1, 6 * D_MODEL), lambda i: (i // tiles_per_mod, 0, 0)),
        pl.BlockSpec((1, D_MODEL), const),
        pl.BlockSpec((D_MODEL, D_FF), const),
        pl.BlockSpec((1, D_FF), const),
        pl.BlockSpec((D_FF, D_MODEL), const),
        pl.BlockSpec((1, D_MODEL), const),
    ]
    args = [x2d, mod, n2, w1, b1, w2, b2]
    if final:
        in_specs.append(pl.BlockSpec((1, D_MODEL), const))
        args.append(final_norm)
    return pl.pallas_call(
        functools.partial(_mlp_kernel, final=final),
        grid=(ntok // TOK_TILE,),
        in_specs=in_specs,
        out_specs=pl.BlockSpec((TOK_TILE, D_MODEL), row),
        out_shape=jax.ShapeDtypeStruct((ntok, D_MODEL), F32),
        compiler_params=pltpu.CompilerParams(dimension_semantics=("arbitrary",),
                                             vmem_limit_bytes=VMEM_LIMIT),
        name="mlp",
    )(*args)


def _rope_tables(n_tokens):
    freqs = HEAD // 4
    t = jnp.arange(n_tokens)
    row = (t // GRID_W).astype(F32)
    col = (t % GRID_W).astype(F32)
    inv = ROPE_THETA ** (-jnp.arange(freqs, dtype=F32) / freqs)
    ang = jnp.concatenate([row[:, None] * inv, col[:, None] * inv], axis=-1)
    cos, sin = jnp.cos(ang), jnp.sin(ang)
    cos_h = jnp.concatenate([cos, cos], axis=-1)
    sin_h = jnp.concatenate([-sin, sin], axis=-1)
    reps = LANES // HEAD
    return jnp.tile(cos_h, (1, reps)), jnp.tile(sin_h, (1, reps))


def _pack_w_in(w):
    pad = jnp.zeros((D_MODEL, LANES - 4 * N_DN_HEADS), w.dtype)
    return jnp.concatenate([w[:, :C_AB], w[:, C_AB:C_AB + 4 * N_DN_HEADS], pad,
                            w[:, C_AB + 4 * N_DN_HEADS:]], axis=1).astype(BF16)


def _pad_lanes(v):
    v = v.reshape(1, -1)
    return jnp.pad(v, ((0, 0), (0, LANES - v.shape[1])))


def _expand_selector(direction):
    src = jnp.arange(LANES)[:, None]
    dst = jnp.arange(2 * DN_W)[None, :]
    head = (dst % DN_W) // HEAD
    want = jnp.where(dst < DN_W, direction * N_DN_HEADS + head,
                     2 * N_DN_HEADS + direction * N_DN_HEADS + head)
    return jnp.tile((src == want).astype(BF16), (3, 1))


def _kv_head_orders(k, v):
    k2 = jnp.concatenate([k[:, 0], k[:, 1], k[:, 1], k[:, 0]], axis=-1)
    vt = jnp.concatenate([v[:, 0], v[:, 1], v[:, 1], v[:, 0]], axis=-1).transpose(0, 2, 1)
    return k2.astype(BF16), vt.astype(BF16)


def _state_to_block_diag(s):
    b = s.shape[0]
    s = s.reshape(b, DN_W // GROUP_W, SEGS, HEAD, HEAD).astype(F32)
    eye = jnp.eye(SEGS, dtype=F32)
    bd = s[:, :, :, :, None, :] * eye[None, None, :, None, :, None]
    return bd.reshape(b, DN_W // GROUP_W, GROUP_W, GROUP_W)


def _state_from_block_diag(bd):
    b = bd.shape[0]
    x = bd.reshape(b, DN_W // GROUP_W, SEGS, HEAD, SEGS, HEAD)
    s = jnp.stack([x[:, :, i, :, i, :] for i in range(SEGS)], axis=2)
    return s.reshape(b, N_DN_HEADS, HEAD, HEAD)


def kernel(x_prompt, x_sample, c, cache_k, cache_v, state_delta, c_ctx, w_mod, b_mod, norm1, norm2,
           w_in, conv_w, q_gain, k_gain, a_log, dt_bias, dn_gain, w_pa, w_pd, w_out, w1, b1, w2, b2,
           final_norm):
    depth = w_in.shape[0]
    batch, seq, _ = x_prompt.shape
    dec_batch, dec_seq, _ = x_sample.shape
    assert seq == TOK_TILE and dec_seq % TOK_TILE == 0

    mod_rows = 8
    cvec = jnp.concatenate([c, c_ctx[None, :],
                            jnp.zeros((mod_rows - dec_batch - 1, D_MODEL), F32)], axis=0)
    mod_all = _modulation(cvec, w_mod.astype(BF16), b_mod)

    ones_bd = (jnp.arange(GROUP_W)[:, None] // HEAD == jnp.arange(GROUP_W)[None, :] // HEAD
               ).astype(BF16)
    esel = [_expand_selector(0), _expand_selector(1)]
    rope_tabs = _rope_tables(dec_seq)

    xp = x_prompt.reshape(batch * seq, D_MODEL)
    xs = x_sample.reshape(dec_batch * dec_seq, D_MODEL)
    ks_out, vs_out, st_out = [], [], []
    for l in range(depth):
        w_cat = _pack_w_in(w_in[l])
        wvt = w_in[l][:, ATTN_W + KV_W:ATTN_W + 2 * KV_W].T.astype(BF16)
        n1 =norm1[l].reshape(1, D_MODEL)
        n2 = norm2[l].reshape(1, D_MODEL)
        qg = jnp.tile(q_gain[l], N_Q_HEADS).reshape(1, ATTN_W)
        kg = jnp.tile(k_gain[l], N_KV_HEADS).reshape(1, KV_W)
        dng = jnp.tile(dn_gain[l], N_DN_HEADS).reshape(1, DN_W)
        alog = _pad_lanes(a_log[l])
        dtb = _pad_lanes(dt_bias[l])
        wpa, wpd, wout = w_pa[l].astype(BF16), w_pd[l].astype(BF16), w_out[l].astype(BF16)
        w1b, w2b = w1[l].astype(BF16), w2[l].astype(BF16)
        b1r, b2r = b1[l].reshape(1, D_FF), b2[l].reshape(1, D_MODEL)
        last = l == depth - 1
        fin = final_norm.reshape(1, D_MODEL) if last else None
        mod_lat = mod_all[l, :dec_batch].reshape(dec_batch, 1, 6 * D_MODEL)
        mod_ctx = mod_all[l, dec_batch:dec_batch + 1].reshape(1, 1, 6 * D_MODEL)

        tpm = batch * seq // TOK_TILE
        q, k2, vt, dcat, ab, gates, k_c, v_c = _inproj(xp, mod_ctx, tpm, n1, w_cat, wvt, qg, kg,
                                                       alog, dtb, ones_bd, None, True, seq)
        oa = _attention(q, k2, vt, None, None, batch, seq)
        o_f, s_f = _deltanet(dcat, ab, conv_w[l], ones_bd, esel[0], None, batch, seq, True, True)
        o_b, s_b = _deltanet(dcat, ab, conv_w[l], ones_bd, esel[1], None, batch, seq, False, True)
        x1 = _post(xp, oa, o_f, o_b, dcat, gates, mod_ctx, tpm, wpa, wpd, wout, dng, ones_bd)
        xp = _mlp(x1, mod_ctx, tpm, n2, w1b, b1r, w2b, b2r, fin)
        ks_out.append(k_c)
        vs_out.append(v_c)
        st_out.append(jnp.stack([_state_from_block_diag(s_f), _state_from_block_diag(s_b)], axis=1))

        tpm = dec_seq // TOK_TILE
        k2c, vtc = _kv_head_orders(cache_k[:, l], cache_v[:, l])
        q, k2, vt, dcat, ab, gates = _inproj(xs, mod_lat, tpm, n1, w_cat, wvt, qg, kg, alog, dtb,
                                             ones_bd, rope_tabs, False, dec_seq)
        oa = _attention(q, k2, vt, k2c, vtc, dec_batch, dec_seq)
        s0f = _state_to_block_diag(state_delta[:, l, 0])
        s0b = _state_to_block_diag(state_delta[:, l, 1])
        (o_f,) = _deltanet(dcat, ab, conv_w[l], ones_bd, esel[0], s0f, dec_batch, dec_seq, True, False)
        (o_b,) = _deltanet(dcat, ab, conv_w[l], ones_bd, esel[1], s0b, dec_batch, dec_seq, False, False)
        x1 = _post(xs, oa, o_f, o_b, dcat, gates, mod_lat, tpm, wpa, wpd, wout, dng, ones_bd)
        xs = _mlp(x1, mod_lat, tpm, n2, w1b, b1r, w2b, b2r, fin)

    y_prompt = xp.reshape(batch, seq, D_MODEL)
    y_sample = xs.reshape(dec_batch, dec_seq, D_MODEL)
    return (y_prompt, y_sample, jnp.stack(ks_out, axis=1), jnp.stack(vs_out, axis=1),
            jnp.stack(st_out, axis=1))
```

```python
import functools

import jax
import jax.numpy as jnp
from jax import lax
from jax.experimental import pallas as pl
from jax.experimental.pallas import tpu as pltpu

F32 = jnp.float32
BF16 = jnp.bfloat16

D_MODEL = 1024
HEAD = 64
N_Q_HEADS = 8
N_KV_HEADS = 2
N_DN_HEADS = 8
ATTN_W = N_Q_HEADS * HEAD
KV_W = N_KV_HEADS * HEAD
DN_W = N_DN_HEADS * HEAD
CHUNK = 64
GRID_W = 64
D_FF = 4 * D_MODEL
ROPE_THETA = 10000.0
EPS = 1e-6
LOG2_E = 1.4426950408889634

LANES = 128
GROUP_W = 256
SEGS = GROUP_W // HEAD
TOK_TILE = 256
VT_ROWS = 2 * N_KV_HEADS * KV_W
VMEM_LIMIT = 56 * 1024 * 1024

C_A = 0
C_D = ATTN_W + 2 * KV_W
C_AB = C_D + 4 * DN_W
C_G = C_AB + LANES
C_END = C_G + 2 * D_MODEL


def _sigmoid(x):
    return 1.0 / (1.0 + jnp.exp(-x))


def _bdot(a, b):
    return jnp.dot(a.astype(BF16), b.astype(BF16), preferred_element_type=F32)


def _split2(x):
    hi = x.astype(BF16)
    return hi, (x - hi.astype(F32)).astype(BF16)


def _split3(x):
    hi = x.astype(BF16)
    r = x - hi.astype(F32)
    mid = r.astype(BF16)
    return hi, mid, (r - mid.astype(F32)).astype(BF16)


def _sel_dot_left(x, sel3):
    return jnp.dot(jnp.concatenate(_split3(x), axis=1), sel3, preferred_element_type=F32)


def _sel_dot_right(sel3, x):
    return jnp.dot(sel3, jnp.concatenate(_split3(x), axis=0), preferred_element_type=F32)


def _seg_sumsq(x, ones_bd):
    x2 = x * x
    hi = x2.astype(BF16)
    lo = (x2 - hi.astype(F32)).astype(BF16)
    width = x.shape[1]
    if width == LANES:
        m = ones_bd[:LANES, :LANES]
        return (jnp.dot(hi, m, preferred_element_type=F32)
                + jnp.dot(lo, m, preferred_element_type=F32))
    outs = []
    for g in range(width // GROUP_W):
        sl = slice(g * GROUP_W, (g + 1) * GROUP_W)
        outs.append(jnp.dot(hi[:, sl], ones_bd, preferred_element_type=F32)
                    + jnp.dot(lo[:, sl], ones_bd, preferred_element_type=F32))
    return jnp.concatenate(outs, axis=1)


def _rms_rows(x):
    return x * lax.rsqrt(jnp.mean(x * x, axis=-1, keepdims=True) + EPS)


def _mod_kernel(c_ref, w_ref, b_ref, o_ref):
    c = c_ref[...]
    s = c * _sigmoid(c)
    o_ref[0] = jnp.dot(s.astype(BF16), w_ref[0], preferred_element_type=F32) + b_ref[0]


def _modulation(cvec, w_mod_bf, b_mod):
    depth = w_mod_bf.shape[0]
    rows = cvec.shape[0]
    nblk = 6 * D_MODEL // D_MODEL
    return pl.pallas_call(
        _mod_kernel,
        grid=(depth, nblk),
        in_specs=[
            pl.BlockSpec((rows, D_MODEL), lambda l, j: (0, 0)),
            pl.BlockSpec((1, D_MODEL, D_MODEL), lambda l, j: (l, 0, j)),
            pl.BlockSpec((1, 1, D_MODEL), lambda l, j: (l, 0, j)),
        ],
        out_specs=pl.BlockSpec((1, rows, D_MODEL), lambda l, j: (l, 0, j)),
        out_shape=jax.ShapeDtypeStruct((depth, rows, 6 * D_MODEL), F32),
        name="adaln_mod",
    )(cvec, w_mod_bf, b_mod.reshape(depth, 1, 6 * D_MODEL))


def _rope(x, cos, sin_signed):
    width = x.shape[1]
    lane = lax.broadcasted_iota(jnp.int32, x.shape, 1)
    first_half = (lane % HEAD) < (HEAD // 2)
    rot = jnp.where(first_half, pltpu.roll(x, width - HEAD // 2, 1), pltpu.roll(x, HEAD // 2, 1))
    return x * cos + rot * sin_signed


def _inproj_kernel(*refs, rope, cache_out):
    (x_ref, mod_ref, n1_ref, w_ref, wvt_ref, qg_ref, kg_ref, alog_ref, dtb_ref,
     ones_ref) = refs[:10]
    pos = 10
    if rope:
        cos_ref, sin_ref = refs[pos:pos + 2]
        pos += 2
    q_ref, k2_ref, vt_ref, d_ref, ab_ref, g_ref = refs[pos:pos + 6]
    pos += 6
    if cache_out:
        kc_ref, vc_ref = refs[pos:pos + 2]

    shift1 = mod_ref[0, :, 0:D_MODEL]
    scale1 = mod_ref[0, :, D_MODEL:2 * D_MODEL]
    h = _rms_rows(x_ref[...]) * n1_ref[...]
    h = h * (1.0 + scale1) + shift1
    h16 = h.astype(BF16)
    y = jnp.dot(h16, w_ref[...], preferred_element_type=F32)
    ones_bd = ones_ref[...]

    vt = lax.dot_general(wvt_ref[...], h16, (((1,), (1,)), ((), ())),
                         preferred_element_type=F32).astype(BF16)
    ones = jnp.ones((HEAD, vt.shape[1]), BF16)
    for kv in range(N_KV_HEADS):
        v_kv = vt[kv * HEAD:(kv + 1) * HEAD]
        vt_ref[0, (2 * kv) * KV_W:(2 * kv + 1) * KV_W, :] = jnp.concatenate([v_kv, ones], axis=0)
        vt_ref[0, (2 * kv + 1) * KV_W:(2 * kv + 2) * KV_W, :] = jnp.concatenate([ones, v_kv], axis=0)

    qa = y[:, C_A:C_A + ATTN_W]
    qn = qa * lax.rsqrt(_seg_sumsq(qa, ones_bd) * (1.0 / HEAD) + EPS) * qg_ref[...]
    ka = y[:, C_A + ATTN_W:C_A + ATTN_W + KV_W]
    kn = ka * lax.rsqrt(_seg_sumsq(ka, ones_bd) * (1.0 / HEAD) + EPS) * kg_ref[...]
    va = y[:, C_A + ATTN_W + KV_W:C_D]
    if cache_out:
        kc_ref[0, 0] = kn[:, :HEAD]
        kc_ref[0, 1] = kn[:, HEAD:]
        vc_ref[0, 0] = va[:, :HEAD]
        vc_ref[0, 1] = va[:, HEAD:]
    if rope:
        cos = cos_ref[...]
        sin = sin_ref[...]
        qn = _rope(qn, jnp.concatenate([cos] * (ATTN_W // LANES), axis=1),
                   jnp.concatenate([sin] * (ATTN_W // LANES), axis=1))
        kn = _rope(kn, cos, sin)
    q_ref[...] = (qn * (HEAD ** -0.5 * LOG2_E)).astype(BF16)
    k2_ref[0, :, 0:LANES] = kn.astype(BF16)
    k2_ref[0, :, LANES:2 * LANES] = pltpu.roll(kn, HEAD, 1).astype(BF16)

    d_ref[...] = y[:, C_D:C_AB]

    z = y[:, C_AB:C_G]
    zb = z + dtb_ref[...]
    softplus = jnp.maximum(zb, 0.0) + jnp.log(1.0 + jnp.exp(-jnp.abs(zb)))
    log_a = -jnp.exp(alog_ref[...]) * softplus
    lane = lax.broadcasted_iota(jnp.int32, z.shape, 1)
    ab_ref[...] = jnp.where(lane < 2 * N_DN_HEADS, log_a,
                            jnp.where(lane < 4 * N_DN_HEADS, _sigmoid(z), 0.0))

    g_ref[...] = _sigmoid(y[:, C_G:C_END])


def _inproj(x2d, mod, tiles_per_mod, n1, w_cat, wvt, qg, kg, alog, dtb, ones_bd, rope_tabs,
            cache_out, seq_len):
    ntok = x2d.shape[0]
    ntiles = ntok // TOK_TILE
    rope = rope_tabs is not None
    const = lambda i: (0, 0)
    row = lambda i: (i, 0)
    in_specs = [
        pl.BlockSpec((TOK_TILE, D_MODEL), row),
        pl.BlockSpec((1, 1, 6 * D_MODEL), lambda i: (i // tiles_per_mod, 0, 0)),
        pl.BlockSpec((1, D_MODEL), const),
        pl.BlockSpec((D_MODEL, C_END), const),
        pl.BlockSpec((KV_W, D_MODEL), const),
        pl.BlockSpec((1, ATTN_W), const),
        pl.BlockSpec((1, KV_W), const),
        pl.BlockSpec((1, LANES), const),
        pl.BlockSpec((1, LANES), const),
        pl.BlockSpec((GROUP_W, GROUP_W), const),
    ]
    args = [x2d, mod, n1, w_cat, wvt, qg, kg, alog, dtb, ones_bd]
    if rope:
        tiles_per_seq = seq_len // TOK_TILE
        in_specs += [pl.BlockSpec((TOK_TILE, LANES), lambda i: (i % tiles_per_seq, 0))] * 2
        args += list(rope_tabs)
    out_specs = [
        pl.BlockSpec((TOK_TILE, ATTN_W), row),
        pl.BlockSpec((1, TOK_TILE, 2 * LANES), lambda i: (i, 0, 0)),
        pl.BlockSpec((1, VT_ROWS, TOK_TILE), lambda i: (i, 0, 0)),
        pl.BlockSpec((TOK_TILE, 4 * DN_W), row),
        pl.BlockSpec((TOK_TILE, LANES), row),
        pl.BlockSpec((TOK_TILE, 2 * D_MODEL), row),
    ]
    out_shape = [
        jax.ShapeDtypeStruct((ntok, ATTN_W), BF16),
        jax.ShapeDtypeStruct((ntiles, TOK_TILE, 2 * LANES), BF16),
        jax.ShapeDtypeStruct((ntiles, VT_ROWS, TOK_TILE), BF16),
        jax.ShapeDtypeStruct((ntok, 4 * DN_W), F32),
        jax.ShapeDtypeStruct((ntok, LANES), F32),
        jax.ShapeDtypeStruct((ntok, 2 * D_MODEL), F32),
    ]
    if cache_out:
        assert seq_len == TOK_TILE
        out_specs += [pl.BlockSpec((1, N_KV_HEADS, TOK_TILE, HEAD), lambda i: (i, 0, 0, 0))] * 2
        out_shape += [jax.ShapeDtypeStruct((ntiles, N_KV_HEADS, TOK_TILE, HEAD), F32)] * 2
    return pl.pallas_call(
        functools.partial(_inproj_kernel, rope=rope, cache_out=cache_out),
        grid=(ntiles,),
        in_specs=in_specs,
        out_specs=out_specs,
        out_shape=out_shape,
        compiler_params=pltpu.CompilerParams(dimension_semantics=("arbitrary",),
                                             vmem_limit_bytes=VMEM_LIMIT),
        name="inproj",
    )(*args)


def _attn_kernel(*refs, has_ctx):
    if has_ctx:
        q_ref, k2_ref, vt_ref, k2c_ref, vtc_ref, o_ref, s_scr, p_scr, acc_scr = refs
    else:
        q_ref, k2_ref, vt_ref, o_ref, s_scr, p_scr, acc_scr = refs
    nchunks = k2_ref.shape[0]
    tq = q_ref.shape[0]
    lane = lax.broadcasted_iota(jnp.int32, (1, LANES), 1)
    lower = lane < HEAD
    nt = (((1,), (1,)), ((), ()))
    qh, sel, vsel, den_row = [], [], [], []
    for head in range(N_Q_HEADS):
        j, half = divmod(head, 2)
        kv = head // (N_Q_HEADS // N_KV_HEADS)
        qb = q_ref[:, j * LANES:(j + 1) * LANES]
        qh.append(jnp.where(lower if half == 0 else jnp.logical_not(lower), qb, jnp.zeros_like(qb)))
        sel.append(0 if half == kv else 1)
        vsel.append(2 * kv + half)
        den_row.append((1 - half) * HEAD)

    heads = range(N_Q_HEADS)

    def k_of(c):
        return k2c_ref[0] if (has_ctx and isinstance(c, int) and c < 0) else k2_ref[c]

    def vt_of(c):
        return vtc_ref[0] if (has_ctx and isinstance(c, int) and c < 0) else vt_ref[c]

    def put_scores(c, slot):
        k_chunk = k_of(c)
        for h in heads:
            s_scr[slot, h] = lax.dot_general(k_chunk[:, sel[h] * LANES:(sel[h] + 1) * LANES], qh[h],
                                             nt, preferred_element_type=F32)

    def add_values(c, slot, alpha):
        vt_chunk = vt_of(c)
        for h in heads:
            pv = jnp.dot(vt_chunk[vsel[h] * KV_W:(vsel[h] + 1) * KV_W, :], p_scr[slot, h],
                         preferred_element_type=F32)
            acc_scr[h] = alpha[h] * acc_scr[h] + pv

    def softmax(slot, stats):
        new_stats, alpha = [], []
        for h in heads:
            sx = s_scr[slot, h]
            m_new = jnp.maximum(stats[h], jnp.max(sx, axis=0, keepdims=True))
            p_scr[slot, h] = jnp.exp2(sx - m_new).astype(BF16)
            alpha.append(jnp.exp2(stats[h] - m_new))
            new_stats.append(m_new)
        return tuple(new_stats), tuple(alpha)

    def step(c, parity, stats, alpha, has_prev, has_next):
        if has_next:
            put_scores(c + 1, 1 - parity)
        if has_prev:
            add_values(c - 1, 1 - parity, alpha)
        return softmax(parity, stats)

    first = -1 if has_ctx else 0
    last = nchunks - 1
    stats = tuple(jnp.full((1, tq), -1e30, F32) for _ in heads)
    alpha = tuple(jnp.zeros((1, tq), F32) for _ in heads)
    acc_scr[...] = jnp.zeros_like(acc_scr)
    put_scores(first, first % 2)
    c = first
    while c <= min(0, last):
        stats, alpha = step(c, c % 2, stats, alpha, c > first, c < last)
        c += 1
    npairs = max(last - c, 0) // 2

    def body(i, carry):
        stats, alpha = carry
        c0 = c + 2 * i
        stats, alpha = step(c0, c % 2, stats, alpha, True, True)
        return step(c0 + 1, (c + 1) % 2, stats, alpha, True, True)

    if npairs > 0:
        stats, alpha = lax.fori_loop(0, npairs, body, (stats, alpha))
        c += 2 * npairs
    while c <= last:
        stats, alpha = step(c, c % 2, stats, alpha, c > first, c < last)
        c += 1
    add_values(last, last % 2, alpha)
    res = []
    for h in heads:
        acc = acc_scr[h]
        res.append((acc / acc[den_row[h]:den_row[h] + 1, :]).T)
    blocks = [jnp.where(lower, res[2 * j], res[2 * j + 1]) for j in range(ATTN_W // LANES)]
    o_ref[...] = jnp.concatenate(blocks, axis=1).astype(BF16)


def _attention(q, k2, vt, k2c, vtc, batch, seq_len):
    ntok = q.shape[0]
    nq = seq_len // TOK_TILE
    has_ctx = k2c is not None
    in_specs = [
        pl.BlockSpec((TOK_TILE, ATTN_W), lambda b, i: (b * nq + i, 0)),
        pl.BlockSpec((nq, TOK_TILE, 2 * LANES), lambda b, i: (b, 0, 0)),
        pl.BlockSpec((nq, VT_ROWS, TOK_TILE), lambda b, i: (b, 0, 0)),
    ]
    args = [q, k2, vt]
    if has_ctx:
        in_specs += [pl.BlockSpec((1,) + k2c.shape[1:], lambda b, i: (b, 0, 0)),
                     pl.BlockSpec((1,) + vtc.shape[1:], lambda b, i: (b, 0, 0))]
        args += [k2c, vtc]
    return pl.pallas_call(
        functools.partial(_attn_kernel, has_ctx=has_ctx),
        grid=(batch, nq),
        in_specs=in_specs,
        out_specs=pl.BlockSpec((TOK_TILE, ATTN_W), lambda b, i: (b * nq + i, 0)),
        out_shape=jax.ShapeDtypeStruct((ntok, ATTN_W), BF16),
        scratch_shapes=[pltpu.VMEM((2, N_Q_HEADS, TOK_TILE, TOK_TILE), F32),
                        pltpu.VMEM((2, N_Q_HEADS, TOK_TILE, TOK_TILE), BF16),
                        pltpu.VMEM((N_Q_HEADS, KV_W, TOK_TILE), F32)],
        compiler_params=pltpu.CompilerParams(dimension_semantics=("arbitrary", "arbitrary"),
                                             vmem_limit_bytes=VMEM_LIMIT),
        name="attention",
    )(*args)


def _block_diag(x):
    seg = lax.broadcasted_iota(jnp.int32, x.shape, 1) // HEAD
    return jnp.concatenate([jnp.where(seg == s, x, jnp.zeros_like(x)) for s in range(SEGS)], axis=0)


def _dn_kernel(*refs, fwd, nblk, has_state, want_state):
    x_ref, xp_ref, xn_ref, ab_ref, cw_ref, ones_ref, esel_ref = refs[:7]
    pos = 7
    if has_state:
        s0_ref = refs[pos]
        pos += 1
    o_ref = refs[pos]
    pos += 1
    if want_state:
        sf_ref = refs[pos]
        pos += 1
    s_scr = refs[pos]

    tb = TOK_TILE
    ngroups = DN_W // GROUP_W
    blk = pl.program_id(1)
    rb = blk if fwd else nblk - 1 - blk

    @pl.when(blk == 0)
    def _():
        if has_state:
            s_scr[...] = s0_ref[0]
        else:
            s_scr[...] = jnp.zeros_like(s_scr)

    x = x_ref[...]
    rows = lax.broadcasted_iota(jnp.int32, (tb, 1), 0)
    prev = jnp.where(rb > 0, xp_ref[7:8, :], 0.0)
    nxt = jnp.where(rb < nblk - 1, xn_ref[0:1, :], 0.0)
    xm1 = jnp.where(rows == 0, prev, pltpu.roll(x, 1, 0))
    xp1 = jnp.where(rows == tb - 1, nxt, pltpu.roll(x, tb - 1, 0))
    y = xm1 * cw_ref[0:1, :] + x * cw_ref[1:2, :] + xp1 * cw_ref[2:3, :]
    y = y * _sigmoid(y)

    ones_bd = ones_ref[...]
    q = y[:, 0:DN_W]
    k = y[:, DN_W:2 * DN_W]
    v = y[:, 2 * DN_W:3 * DN_W]
    q = q * lax.rsqrt(_seg_sumsq(q, ones_bd) + EPS) * (HEAD ** -0.5)
    k = k * lax.rsqrt(_seg_sumsq(k, ones_bd) + EPS)

    ex = _sel_dot_left(ab_ref[...], esel_ref[...])
    la = ex[:, 0:DN_W]
    be = ex[:, DN_W:2 * DN_W]
    same_chunk = ones_bd.astype(F32)
    r_i = lax.broadcasted_iota(jnp.int32, (tb, tb), 0) % CHUNK
    c_i = lax.broadcasted_iota(jnp.int32, (tb, tb), 1) % CHUNK
    tri = jnp.where((c_i <= r_i) if fwd else (c_i >= r_i), ones_bd, jnp.zeros_like(ones_bd))
    g_col = _sel_dot_right(jnp.concatenate([tri] * 3, axis=1), la)
    t_i = lax.broadcasted_iota(jnp.int32, (tb, DN_W), 0) % CHUNK
    l_i = lax.broadcasted_iota(jnp.int32, (tb, DN_W), 1) % CHUNK
    g_row = _sel_dot_right(jnp.concatenate([ones_bd] * 3, axis=1),
                           jnp.where((t_i <= l_i) if fwd else (t_i >= l_i), la, 0.0))

    ii = lax.broadcasted_iota(jnp.int32, (CHUNK, GROUP_W), 0)
    jj = lax.broadcasted_iota(jnp.int32, (CHUNK, GROUP_W), 1) % HEAD
    incl = (jj <= ii) if fwd else (jj >= ii)
    strict = (jj < ii) if fwd else (jj > ii)
    eye = jnp.where(jj == ii, 1.0, 0.0)
    nt = (((1,), (1,)), ((), ()))
    tn = (((0,), (0,)), ((), ()))
    nchunk = tb // CHUNK

    order = [c if fwd else nchunk - 1 - c for c in range(nchunk)]
    pairs = [(slice(c * CHUNK, (c + 1) * CHUNK), slice(g * GROUP_W, (g + 1) * GROUP_W), g)
             for c in order for g in range(ngroups)]
    dot = functools.partial(jnp.dot, preferred_element_type=F32)

    dec, kq = [], []
    for rs, ls, _ in pairs:
        dec.append(jnp.exp(jnp.where(incl, g_col[rs, ls] - g_row[rs, ls], -1e30)))
        kc16 = k[rs, ls].astype(BF16)
        kq.append(lax.dot_general(jnp.concatenate([kc16, q[rs, ls].astype(BF16)], axis=0),
                                  _block_diag(kc16), nt, preferred_element_type=F32))

    neg_l = [-jnp.where(strict, be[rs, ls] * kqp[:CHUNK] * d, 0.0)
             for (rs, ls, _), kqp, d in zip(pairs, kq, dec)]
    acc = [eye + n for n in neg_l]
    power = list(neg_l)
    n_doublings = CHUNK.bit_length() - 3
    assert 2 ** (n_doublings + 2) == CHUNK
    for it in range(n_doublings + 1):
        for p in range(len(pairs)):
            wmat = _block_diag(power[p].astype(BF16))
            if it == 0:
                power[p] = dot(power[p].astype(BF16), wmat)
            elif it < n_doublings:
                r = dot(jnp.concatenate([acc[p], power[p]], axis=0).astype(BF16), wmat)
                acc[p] = acc[p] + r[:CHUNK]
                power[p] = r[CHUNK:]
            else:
                acc[p] = acc[p] + dot(acc[p].astype(BF16), wmat)

    xs = [_split2(a) for a in acc]
    ns = [_split2(n) for n in neg_l]
    nx1 = [dot(jnp.concatenate([nh, nl], axis=0), _block_diag(xh))
           for (nh, nl), (xh, _) in zip(ns, xs)]
    nx2 = [dot(nh, _block_diag(xl)) for (nh, _), (_, xl) in zip(ns, xs)]
    resid = [eye - a + (n1[:CHUNK] + n1[CHUNK:] + n2) for a, n1, n2 in zip(acc, nx1, nx2)]
    acc = [a + dot(xh, _block_diag(rsd.astype(BF16))) for a, (xh, _), rsd in zip(acc, xs, resid)]

    eg_c = [jnp.exp(g_col[rs, ls]) for rs, ls, _ in pairs]
    wu = []
    for (rs, ls, _), a, eg in zip(pairs, acc, eg_c):
        rw = (be[rs, ls] * eg * k[rs, ls]).astype(BF16)
        ru = (be[rs, ls] * v[rs, ls]).astype(BF16)
        wu.append(dot(a.astype(BF16), jnp.concatenate([_block_diag(rw), _block_diag(ru)], axis=1)))

    lhs, qk_d, k_dec, eg_last = [], [], [], []
    for (rs, ls, _), kqp, d, eg, wup in zip(pairs, kq, dec, eg_c, wu):
        gc = g_col[rs, ls]
        g_last = gc[CHUNK - 1:CHUNK, :] if fwd else gc[0:1, :]
        lhs.append(jnp.concatenate([wup[:, :GROUP_W], q[rs, ls] * eg], axis=0).astype(BF16))
        qk_d.append((kqp[CHUNK:] * d).astype(BF16))
        k_dec.append((k[rs, ls] * jnp.exp(g_last - gc)).astype(BF16))
        eg_last.append(jnp.exp(g_last))

    for step in range(nchunk):
        ps = range(step * ngroups, (step + 1) * ngroups)
        s_bd = [s_scr[pairs[p][2]] for p in ps]
        r = [dot(lhs[p], s.astype(BF16)) for p, s in zip(ps, s_bd)]
        u16 = [(wu[p][:, GROUP_W:] - rp[:CHUNK]).astype(BF16) for p, rp in zip(ps, r)]
        o_c = [rp[CHUNK:] + dot(qk_d[p], _block_diag(u)) for p, rp, u in zip(ps, r, u16)]
        upd = [lax.dot_general(k_dec[p], u, tn, preferred_element_type=F32) for p, u in zip(ps, u16)]
        for p, s, o, up in zip(ps, s_bd, o_c, upd):
            rs, ls, g = pairs[p]
            s_scr[g] = s * eg_last[p] + same_chunk * up
            o_ref[rs, ls] = o

    if want_state:
        @pl.when(blk == nblk - 1)
        def _():
            for g in range(ngroups):
                for s in range(SEGS):
                    rows = s_scr[g, s * HEAD:(s + 1) * HEAD, :]
                    sf_ref[0, g * SEGS + s] = rows[:, s * HEAD:(s + 1) * HEAD]


def _deltanet(dcat, ab, conv_w, ones_bd, esel, s0_bd, batch, seq_len, fwd, want_state):
    ntok = dcat.shape[0]
    nblk = seq_len // TOK_TILE
    has_state = s0_bd is not None
    ngroups = DN_W // GROUP_W
    sub = TOK_TILE // 8

    def blkpos(b, i):
        return b * nblk + (i if fwd else nblk - 1 - i)

    in_specs = [
        pl.BlockSpec((TOK_TILE, 3 * DN_W), lambda b, i: (blkpos(b, i), 0)),
        pl.BlockSpec((8, 3 * DN_W), lambda b, i: (jnp.maximum(blkpos(b, i) * sub - 1, 0), 0)),
        pl.BlockSpec((8, 3 * DN_W),
                     lambda b, i: (jnp.minimum((blkpos(b, i) + 1) * sub, ntok // 8 - 1), 0)),
        pl.BlockSpec((TOK_TILE, LANES), lambda b, i: (blkpos(b, i), 0)),
        pl.BlockSpec((3, 3 * DN_W), lambda b, i: (0, 0)),
        pl.BlockSpec((GROUP_W, GROUP_W), lambda b, i: (0, 0)),
        pl.BlockSpec((3 * LANES, 2 * DN_W), lambda b, i: (0, 0)),
    ]
    args = [dcat, dcat, dcat, ab, conv_w, ones_bd, esel]
    if has_state:
        in_specs.append(pl.BlockSpec((1, ngroups, GROUP_W, GROUP_W), lambda b, i: (b, 0, 0, 0)))
        args.append(s0_bd)
    out_specs = [pl.BlockSpec((TOK_TILE, DN_W), lambda b, i: (blkpos(b, i), 0))]
    out_shape = [jax.ShapeDtypeStruct((ntok, DN_W), F32)]
    if want_state:
        out_specs.append(pl.BlockSpec((1, N_DN_HEADS, HEAD, HEAD), lambda b, i: (b, 0, 0, 0)))
        out_shape.append(jax.ShapeDtypeStruct((batch, N_DN_HEADS, HEAD, HEAD), F32))
    return pl.pallas_call(
        functools.partial(_dn_kernel, fwd=fwd, nblk=nblk, has_state=has_state,
                          want_state=want_state),
        grid=(batch, nblk),
        in_specs=in_specs,
        out_specs=out_specs,
        out_shape=out_shape,
        scratch_shapes=[pltpu.VMEM((ngroups, GROUP_W, GROUP_W), F32)],
        compiler_params=pltpu.CompilerParams(dimension_semantics=("arbitrary", "arbitrary"),
                                             vmem_limit_bytes=VMEM_LIMIT),
        name="deltanet_fwd" if fwd else "deltanet_bwd",
    )(*args)


def _post_kernel(x_ref, oa_ref, of_ref, ob_ref, gout_ref, gates_ref, mod_ref, wpa_ref, wpd_ref,
                 wout_ref, dng_ref, ones_ref, o_ref):
    od = of_ref[...] + ob_ref[...]
    od = od * lax.rsqrt(_seg_sumsq(od, ones_ref[...]) * (1.0 / HEAD) + EPS) * dng_ref[...]
    gout = gout_ref[...]
    od = od * (gout * _sigmoid(gout))
    pa = jnp.dot(oa_ref[...], wpa_ref[...], preferred_element_type=F32)
    pd = jnp.dot(od.astype(BF16), wpd_ref[...], preferred_element_type=F32)
    merged = gates_ref[:, 0:D_MODEL] * pa + gates_ref[:, D_MODEL:2 * D_MODEL] * pd
    gate1 = mod_ref[0, :, 2 * D_MODEL:3 * D_MODEL]
    o_ref[...] = x_ref[...] + gate1 * jnp.dot(merged.astype(BF16), wout_ref[...],
                                              preferred_element_type=F32)


def _post(x2d, oa, o_f, o_b, dcat, gates, mod, tiles_per_mod, wpa, wpd, wout, dng, ones_bd):
    ntok = x2d.shape[0]
    row = lambda i: (i, 0)
    const = lambda i: (0, 0)
    return pl.pallas_call(
        _post_kernel,
        grid=(ntok // TOK_TILE,),
        in_specs=[
            pl.BlockSpec((TOK_TILE, D_MODEL), row),
            pl.BlockSpec((TOK_TILE, ATTN_W), row),
            pl.BlockSpec((TOK_TILE, DN_W), row),
            pl.BlockSpec((TOK_TILE, DN_W), row),
            pl.BlockSpec((TOK_TILE, DN_W), lambda i: (i, 3)),
            pl.BlockSpec((TOK_TILE, 2 * D_MODEL), row),
            pl.BlockSpec((1, 1, 6 * D_MODEL), lambda i: (i // tiles_per_mod, 0, 0)),
            pl.BlockSpec((ATTN_W, D_MODEL), const),
            pl.BlockSpec((DN_W, D_MODEL), const),
            pl.BlockSpec((D_MODEL, D_MODEL), const),
            pl.BlockSpec((1, DN_W), const),
            pl.BlockSpec((GROUP_W, GROUP_W), const),
        ],
        out_specs=pl.BlockSpec((TOK_TILE, D_MODEL), row),
        out_shape=jax.ShapeDtypeStruct((ntok, D_MODEL), F32),
        compiler_params=pltpu.CompilerParams(dimension_semantics=("arbitrary",),
                                             vmem_limit_bytes=VMEM_LIMIT),
        name="merge_outproj",
    )(x2d, oa, o_f, o_b, dcat, gates, mod, wpa, wpd, wout, dng, ones_bd)


def _mlp_kernel(*refs, final):
    x_ref, mod_ref, n2_ref, w1_ref, b1_ref, w2_ref, b2_ref = refs[:7]
    if final:
        fn_ref, o_ref = refs[7:9]
    else:
        o_ref = refs[7]
    x = x_ref[...]
    shift2 = mod_ref[0, :, 3 * D_MODEL:4 * D_MODEL]
    scale2 = mod_ref[0, :, 4 * D_MODEL:5 * D_MODEL]
    gate2 = mod_ref[0, :, 5 * D_MODEL:6 * D_MODEL]
    h = _rms_rows(x) * n2_ref[...]
    h = h * (1.0 + scale2) + shift2
    a = jnp.dot(h.astype(BF16), w1_ref[...], preferred_element_type=F32) + b1_ref[...]
    a = jnp.square(jnp.maximum(a, 0.0))
    ff = jnp.dot(a.astype(BF16), w2_ref[...], preferred_element_type=F32) + b2_ref[...]
    out = x + gate2 * ff
    if final:
        out = _rms_rows(out) * fn_ref[...]
    o_ref[...] = out


def _mlp(x2d, mod, tiles_per_mod, n2, w1, b1, w2, b2, final_norm):
    ntok = x2d.shape[0]
    final = final_norm is not None
    row = lambda i: (i, 0)
    const = lambda i: (0, 0)
    in_specs = [
        pl.BlockSpec((TOK_TILE, D_MODEL), row),
        pl.BlockSpec((1, 1, 6 * D_MODEL), lambda i: (i // tiles_per_mod, 0, 0)),
        pl.BlockSpec((1, D_MODEL), const),
        pl.BlockSpec((D_MODEL, D_FF), const),
        pl.BlockSpec((1, D_FF), const),
        pl.BlockSpec((D_FF, D_MODEL), const),
        pl.BlockSpec((1, D_MODEL), const),
    ]
    args = [x2d, mod, n2, w1, b1, w2, b2]
    if final:
        in_specs.append(pl.BlockSpec((1, D_MODEL), const))
        args.append(final_norm)
    return pl.pallas_call(
        functools.partial(_mlp_kernel, final=final),
        grid=(ntok // TOK_TILE,),
        in_specs=in_specs,
        out_specs=pl.BlockSpec((TOK_TILE, D_MODEL), row),
        out_shape=jax.ShapeDtypeStruct((ntok, D_MODEL), F32),
        compiler_params=pltpu.CompilerParams(dimension_semantics=("arbitrary",),
                                             vmem_limit_bytes=VMEM_LIMIT),
        name="mlp",
    )(*args)


def _rope_tables(n_tokens):
    freqs = HEAD // 4
    t = jnp.arange(n_tokens)
    row = (t // GRID_W).astype(F32)
    col = (t % GRID_W).astype(F32)
    inv = ROPE_THETA ** (-jnp.arange(freqs, dtype=F32) / freqs)
    ang = jnp.concatenate([row[:, None] * inv, col[:, None] * inv], axis=-1)
    cos, sin = jnp.cos(ang), jnp.sin(ang)
    cos_h = jnp.concatenate([cos, cos], axis=-1)
    sin_h = jnp.concatenate([-sin, sin], axis=-1)
    reps = LANES // HEAD
    return jnp.tile(cos_h, (1, reps)), jnp.tile(sin_h, (1, reps))


def _pack_w_in(w):
    pad = jnp.zeros((D_MODEL, LANES - 4 * N_DN_HEADS), w.dtype)
    return jnp.concatenate([w[:, :C_AB], w[:, C_AB:C_AB + 4 * N_DN_HEADS], pad,
                            w[:, C_AB + 4 * N_DN_HEADS:]], axis=1).astype(BF16)


def _pad_lanes(v):
    v = v.reshape(1, -1)
    return jnp.pad(v, ((0, 0), (0, LANES - v.shape[1])))


def _expand_selector(direction):
    src = jnp.arange(LANES)[:, None]
    dst = jnp.arange(2 * DN_W)[None, :]
    head = (dst % DN_W) // HEAD
    want = jnp.where(dst < DN_W, direction * N_DN_HEADS + head,
                     2 * N_DN_HEADS + direction * N_DN_HEADS + head)
    return jnp.tile((src == want).astype(BF16), (3, 1))


def _kv_head_orders(k, v):
    k2 = jnp.concatenate([k[:, 0], k[:, 1], k[:, 1], k[:, 0]], axis=-1)
    ones = jnp.ones_like(v[:, 0])
    vt = jnp.concatenate([v[:, 0], ones, ones, v[:, 0], v[:, 1], ones, ones, v[:, 1]], axis=-1)
    return k2.astype(BF16), vt.transpose(0, 2, 1).astype(BF16)


def _state_to_block_diag(s):
    b = s.shape[0]
    s = s.reshape(b, DN_W // GROUP_W, SEGS, HEAD, HEAD).astype(F32)
    eye = jnp.eye(SEGS, dtype=F32)
    bd = s[:, :, :, :, None, :] * eye[None, None, :, None, :, None]
    return bd.reshape(b, DN_W // GROUP_W, GROUP_W, GROUP_W)


def kernel(x_prompt, x_sample, c, cache_k, cache_v, state_delta, c_ctx, w_mod, b_mod, norm1, norm2,
           w_in, conv_w, q_gain, k_gain, a_log, dt_bias, dn_gain, w_pa, w_pd, w_out, w1, b1, w2, b2,
           final_norm):
    depth = w_in.shape[0]
    batch, seq, _ = x_prompt.shape
    dec_batch, dec_seq, _ = x_sample.shape
    assert seq == TOK_TILE and dec_seq % TOK_TILE == 0

    mod_rows = 8
    cvec = jnp.concatenate([c, c_ctx[None, :],
                            jnp.zeros((mod_rows - dec_batch - 1, D_MODEL), F32)], axis=0)
    mod_all = _modulation(cvec, w_mod.astype(BF16), b_mod)

    ones_bd = (jnp.arange(GROUP_W)[:, None] // HEAD == jnp.arange(GROUP_W)[None, :] // HEAD
               ).astype(BF16)
    esel = [_expand_selector(0), _expand_selector(1)]
    rope_tabs = _rope_tables(dec_seq)

    xp = x_prompt.reshape(batch * seq, D_MODEL)
    xs = x_sample.reshape(dec_batch * dec_seq, D_MODEL)
    ks_out, vs_out, st_out = [], [], []
    for l in range(depth):
        w_cat = _pack_w_in(w_in[l])
        wvt = w_in[l][:, ATTN_W + KV_W:ATTN_W + 2 * KV_W].T.astype(BF16)
        n1 =norm1[l].reshape(1, D_MODEL)
        n2 = norm2[l].reshape(1, D_MODEL)
        qg = jnp.tile(q_gain[l], N_Q_HEADS).reshape(1, ATTN_W)
        kg = jnp.tile(k_gain[l], N_KV_HEADS).reshape(1, KV_W)
        dng = jnp.tile(dn_gain[l], N_DN_HEADS).reshape(1, DN_W)
        alog = _pad_lanes(a_log[l])
        dtb = _pad_lanes(dt_bias[l])
        wpa, wpd, wout = w_pa[l].astype(BF16), w_pd[l].astype(BF16), w_out[l].astype(BF16)
        w1b, w2b = w1[l].astype(BF16), w2[l].astype(BF16)
        b1r, b2r = b1[l].reshape(1, D_FF), b2[l].reshape(1, D_MODEL)
        last = l == depth - 1
        fin = final_norm.reshape(1, D_MODEL) if last else None
        mod_lat = mod_all[l, :dec_batch].reshape(dec_batch, 1, 6 * D_MODEL)
        mod_ctx = mod_all[l, dec_batch:dec_batch + 1].reshape(1, 1, 6 * D_MODEL)

        tpm = batch * seq // TOK_TILE
        q, k2, vt, dcat, ab, gates, k_c, v_c = _inproj(xp, mod_ctx, tpm, n1, w_cat, wvt, qg, kg,
                                                       alog, dtb, ones_bd, None, True, seq)
        oa = _attention(q, k2, vt, None, None, batch, seq)
        o_f, s_f = _deltanet(dcat, ab, conv_w[l], ones_bd, esel[0], None, batch, seq, True, True)
        o_b, s_b = _deltanet(dcat, ab, conv_w[l], ones_bd, esel[1], None, batch, seq, False, True)
        x1 = _post(xp, oa, o_f, o_b, dcat, gates, mod_ctx, tpm, wpa, wpd, wout, dng, ones_bd)
        xp = _mlp(x1, mod_ctx, tpm, n2, w1b, b1r, w2b, b2r, fin)
        ks_out.append(k_c)
        vs_out.append(v_c)
        st_out.append(jnp.stack([s_f, s_b], axis=1))

        tpm = dec_seq // TOK_TILE
        k2c, vtc = _kv_head_orders(cache_k[:, l], cache_v[:, l])
        q, k2, vt, dcat, ab, gates = _inproj(xs, mod_lat, tpm, n1, w_cat, wvt, qg, kg, alog, dtb,
                                             ones_bd, rope_tabs, False, dec_seq)
        oa = _attention(q, k2, vt, k2c, vtc, dec_batch, dec_seq)
        s0f = _state_to_block_diag(state_delta[:, l, 0])
        s0b = _state_to_block_diag(state_delta[:, l, 1])
        (o_f,) = _deltanet(dcat, ab, conv_w[l], ones_bd, esel[0], s0f, dec_batch, dec_seq, True, False)
        (o_b,) = _deltanet(dcat, ab, conv_w[l], ones_bd, esel[1], s0b, dec_batch, dec_seq, False, False)
        x1 = _post(xs, oa, o_f, o_b, dcat, gates, mod_lat, tpm, wpa, wpd, wout, dng, ones_bd)
        xs = _mlp(x1, mod_lat, tpm, n2, w1b, b1r, w2b, b2r, fin)

    y_prompt = xp.reshape(batch, seq, D_MODEL)
    y_sample = xs.reshape(dec_batch, dec_seq, D_MODEL)
    return (y_prompt, y_sample, jnp.stack(ks_out, axis=1), jnp.stack(vs_out, axis=1),
            jnp.stack(st_out, axis=1))
```

```python
import functools

import jax
import jax.numpy as jnp
from jax import lax
from jax.experimental import pallas as pl
from jax.experimental.pallas import tpu as pltpu

F32 = jnp.float32
BF16 = jnp.bfloat16

D_MODEL = 1024
HEAD = 64
N_Q_HEADS = 8
N_KV_HEADS = 2
N_DN_HEADS = 8
ATTN_W = N_Q_HEADS * HEAD
KV_W = N_KV_HEADS * HEAD
DN_W = N_DN_HEADS * HEAD
CHUNK = 64
GRID_W = 64
D_FF = 4 * D_MODEL
ROPE_THETA = 10000.0
EPS = 1e-6
LOG2_E = 1.4426950408889634

LANES = 128
GROUP_W = 256
SEGS = GROUP_W // HEAD
TOK_TILE = 256
VT_ROWS = 2 * N_KV_HEADS * KV_W
VMEM_LIMIT = 56 * 1024 * 1024

C_A = 0
C_D = ATTN_W + 2 * KV_W
C_AB = C_D + 4 * DN_W
C_G = C_AB + LANES
C_END = C_G + 2 * D_MODEL


def _sigmoid(x):
    return 1.0 / (1.0 + jnp.exp(-x))


def _bdot(a, b):
    return jnp.dot(a.astype(BF16), b.astype(BF16), preferred_element_type=F32)


def _split2(x):
    hi = x.astype(BF16)
    return hi, (x - hi.astype(F32)).astype(BF16)


def _split3(x):
    hi = x.astype(BF16)
    r = x - hi.astype(F32)
    mid = r.astype(BF16)
    return hi, mid, (r - mid.astype(F32)).astype(BF16)


def _sel_dot_left(x, sel3):
    return jnp.dot(jnp.concatenate(_split3(x), axis=1), sel3, preferred_element_type=F32)


def _sel_dot_right(sel3, x):
    return jnp.dot(sel3, jnp.concatenate(_split3(x), axis=0), preferred_element_type=F32)


def _seg_sumsq(x, ones_bd):
    x2 = x * x
    hi = x2.astype(BF16)
    lo = (x2 - hi.astype(F32)).astype(BF16)
    width = x.shape[1]
    if width == LANES:
        m = ones_bd[:LANES, :LANES]
        return (jnp.dot(hi, m, preferred_element_type=F32)
                + jnp.dot(lo, m, preferred_element_type=F32))
    outs = []
    for g in range(width // GROUP_W):
        sl = slice(g * GROUP_W, (g + 1) * GROUP_W)
        outs.append(jnp.dot(hi[:, sl], ones_bd, preferred_element_type=F32)
                    + jnp.dot(lo[:, sl], ones_bd, preferred_element_type=F32))
    return jnp.concatenate(outs, axis=1)


def _rms_rows(x):
    return x * lax.rsqrt(jnp.mean(x * x, axis=-1, keepdims=True) + EPS)


def _mod_kernel(c_ref, w_ref, b_ref, o_ref):
    c = c_ref[...]
    s = c * _sigmoid(c)
    o_ref[0] = jnp.dot(s.astype(BF16), w_ref[0], preferred_element_type=F32) + b_ref[0]


def _modulation(cvec, w_mod_bf, b_mod):
    depth = w_mod_bf.shape[0]
    rows = cvec.shape[0]
    nblk = 6 * D_MODEL // D_MODEL
    return pl.pallas_call(
        _mod_kernel,
        grid=(depth, nblk),
        in_specs=[
            pl.BlockSpec((rows, D_MODEL), lambda l, j: (0, 0)),
            pl.BlockSpec((1, D_MODEL, D_MODEL), lambda l, j: (l, 0, j)),
            pl.BlockSpec((1, 1, D_MODEL), lambda l, j: (l, 0, j)),
        ],
        out_specs=pl.BlockSpec((1, rows, D_MODEL), lambda l, j: (l, 0, j)),
        out_shape=jax.ShapeDtypeStruct((depth, rows, 6 * D_MODEL), F32),
        name="adaln_mod",
    )(cvec, w_mod_bf, b_mod.reshape(depth, 1, 6 * D_MODEL))


def _rope(x, cos, sin_signed):
    width = x.shape[1]
    lane = lax.broadcasted_iota(jnp.int32, x.shape, 1)
    first_half = (lane % HEAD) < (HEAD // 2)
    rot = jnp.where(first_half, pltpu.roll(x, width - HEAD // 2, 1), pltpu.roll(x, HEAD // 2, 1))
    return x * cos + rot * sin_signed


def _inproj_kernel(*refs, rope, cache_out):
    (x_ref, mod_ref, n1_ref, w_ref, wvt_ref, qg_ref, kg_ref, alog_ref, dtb_ref,
     ones_ref) = refs[:10]
    pos = 10
    if rope:
        cos_ref, sin_ref = refs[pos:pos + 2]
        pos += 2
    q_ref, k2_ref, vt_ref, d_ref, ab_ref, g_ref = refs[pos:pos + 6]
    pos += 6
    if cache_out:
        kc_ref, vc_ref = refs[pos:pos + 2]

    shift1 = mod_ref[0, :, 0:D_MODEL]
    scale1 = mod_ref[0, :, D_MODEL:2 * D_MODEL]
    h = _rms_rows(x_ref[...]) * n1_ref[...]
    h = h * (1.0 + scale1) + shift1
    h16 = h.astype(BF16)
    y = jnp.dot(h16, w_ref[...], preferred_element_type=F32)
    ones_bd = ones_ref[...]

    vt = lax.dot_general(wvt_ref[...], h16, (((1,), (1,)), ((), ())),
                         preferred_element_type=F32).astype(BF16)
    ones = jnp.ones((HEAD, vt.shape[1]), BF16)
    for kv in range(N_KV_HEADS):
        v_kv = vt[kv * HEAD:(kv + 1) * HEAD]
        vt_ref[0, (2 * kv) * KV_W:(2 * kv + 1) * KV_W, :] = jnp.concatenate([v_kv, ones], axis=0)
        vt_ref[0, (2 * kv + 1) * KV_W:(2 * kv + 2) * KV_W, :] = jnp.concatenate([ones, v_kv], axis=0)

    qa = y[:, C_A:C_A + ATTN_W]
    qn = qa * lax.rsqrt(_seg_sumsq(qa, ones_bd) * (1.0 / HEAD) + EPS) * qg_ref[...]
    ka = y[:, C_A + ATTN_W:C_A + ATTN_W + KV_W]
    kn = ka * lax.rsqrt(_seg_sumsq(ka, ones_bd) * (1.0 / HEAD) + EPS) * kg_ref[...]
    va = y[:, C_A + ATTN_W + KV_W:C_D]
    if cache_out:
        kc_ref[0, 0] = kn[:, :HEAD]
        kc_ref[0, 1] = kn[:, HEAD:]
        vc_ref[0, 0] = va[:, :HEAD]
        vc_ref[0, 1] = va[:, HEAD:]
    if rope:
        cos = cos_ref[...]
        sin = sin_ref[...]
        qn = _rope(qn, jnp.concatenate([cos] * (ATTN_W // LANES), axis=1),
                   jnp.concatenate([sin] * (ATTN_W // LANES), axis=1))
        kn = _rope(kn, cos, sin)
    q_ref[...] = (qn * (HEAD ** -0.5 * LOG2_E)).astype(BF16)
    k2_ref[0, :, 0:LANES] = kn.astype(BF16)
    k2_ref[0, :, LANES:2 * LANES] = pltpu.roll(kn, HEAD, 1).astype(BF16)

    d_ref[...] = y[:, C_D:C_AB]

    z = y[:, C_AB:C_G]
    zb = z + dtb_ref[...]
    softplus = jnp.maximum(zb, 0.0) + jnp.log(1.0 + jnp.exp(-jnp.abs(zb)))
    log_a = -jnp.exp(alog_ref[...]) * softplus
    lane = lax.broadcasted_iota(jnp.int32, z.shape, 1)
    ab_ref[...] = jnp.where(lane < 2 * N_DN_HEADS, log_a,
                            jnp.where(lane < 4 * N_DN_HEADS, _sigmoid(z), 0.0))

    g_ref[...] = _sigmoid(y[:, C_G:C_END]).astype(BF16)


def _inproj(x2d, mod, tiles_per_mod, n1, w_cat, wvt, qg, kg, alog, dtb, ones_bd, rope_tabs,
            cache_out, seq_len):
    ntok = x2d.shape[0]
    ntiles = ntok // TOK_TILE
    rope = rope_tabs is not None
    const = lambda i: (0, 0)
    row = lambda i: (i, 0)
    in_specs = [
        pl.BlockSpec((TOK_TILE, D_MODEL), row),
        pl.BlockSpec((1, 1, 6 * D_MODEL), lambda i: (i // tiles_per_mod, 0, 0)),
        pl.BlockSpec((1, D_MODEL), const),
        pl.BlockSpec((D_MODEL, C_END), const),
        pl.BlockSpec((KV_W, D_MODEL), const),
        pl.BlockSpec((1, ATTN_W), const),
        pl.BlockSpec((1, KV_W), const),
        pl.BlockSpec((1, LANES), const),
        pl.BlockSpec((1, LANES), const),
        pl.BlockSpec((GROUP_W, GROUP_W), const),
    ]
    args = [x2d, mod, n1, w_cat, wvt, qg, kg, alog, dtb, ones_bd]
    if rope:
        tiles_per_seq = seq_len // TOK_TILE
        in_specs += [pl.BlockSpec((TOK_TILE, LANES), lambda i: (i % tiles_per_seq, 0))] * 2
        args += list(rope_tabs)
    out_specs = [
        pl.BlockSpec((TOK_TILE, ATTN_W), row),
        pl.BlockSpec((1, TOK_TILE, 2 * LANES), lambda i: (i, 0, 0)),
        pl.BlockSpec((1, VT_ROWS, TOK_TILE), lambda i: (i, 0, 0)),
        pl.BlockSpec((TOK_TILE, 4 * DN_W), row),
        pl.BlockSpec((TOK_TILE, LANES), row),
        pl.BlockSpec((TOK_TILE, 2 * D_MODEL), row),
    ]
    out_shape = [
        jax.ShapeDtypeStruct((ntok, ATTN_W), BF16),
        jax.ShapeDtypeStruct((ntiles, TOK_TILE, 2 * LANES), BF16),
        jax.ShapeDtypeStruct((ntiles, VT_ROWS, TOK_TILE), BF16),
        jax.ShapeDtypeStruct((ntok, 4 * DN_W), F32),
        jax.ShapeDtypeStruct((ntok, LANES), F32),
        jax.ShapeDtypeStruct((ntok, 2 * D_MODEL), BF16),
    ]
    if cache_out:
        assert seq_len == TOK_TILE
        out_specs += [pl.BlockSpec((1, N_KV_HEADS, TOK_TILE, HEAD), lambda i: (i, 0, 0, 0))] * 2
        out_shape += [jax.ShapeDtypeStruct((ntiles, N_KV_HEADS, TOK_TILE, HEAD), F32)] * 2
    return pl.pallas_call(
        functools.partial(_inproj_kernel, rope=rope, cache_out=cache_out),
        grid=(ntiles,),
        in_specs=in_specs,
        out_specs=out_specs,
        out_shape=out_shape,
        compiler_params=pltpu.CompilerParams(dimension_semantics=("arbitrary",),
                                             vmem_limit_bytes=VMEM_LIMIT),
        name="inproj",
    )(*args)


def _attn_kernel(*refs, has_ctx):
    if has_ctx:
        q_ref, k2_ref, vt_ref, k2c_ref, vtc_ref, o_ref, s_scr, p_scr, acc_scr = refs
    else:
        q_ref, k2_ref, vt_ref, o_ref, s_scr, p_scr, acc_scr = refs
    nchunks = k2_ref.shape[0]
    tq = q_ref.shape[0]
    lane = lax.broadcasted_iota(jnp.int32, (1, LANES), 1)
    lower = lane < HEAD
    nt = (((1,), (1,)), ((), ()))
    qh, sel, vsel, den_row = [], [], [], []
    for head in range(N_Q_HEADS):
        j, half = divmod(head, 2)
        kv = head // (N_Q_HEADS // N_KV_HEADS)
        qb = q_ref[:, j * LANES:(j + 1) * LANES]
        qh.append(jnp.where(lower if half == 0 else jnp.logical_not(lower), qb, jnp.zeros_like(qb)))
        sel.append(0 if half == kv else 1)
        vsel.append(2 * kv + half)
        den_row.append((1 - half) * HEAD)

    heads = range(N_Q_HEADS)

    def k_of(c):
        return k2c_ref[0] if (has_ctx and isinstance(c, int) and c < 0) else k2_ref[c]

    def vt_of(c):
        return vtc_ref[0] if (has_ctx and isinstance(c, int) and c < 0) else vt_ref[c]

    def put_scores(c, slot):
        k_chunk = k_of(c)
        for h in heads:
            s_scr[slot, h] = lax.dot_general(k_chunk[:, sel[h] * LANES:(sel[h] + 1) * LANES], qh[h],
                                             nt, preferred_element_type=F32)

    def add_values(c, slot, alpha):
        vt_chunk = vt_of(c)
        for h in heads:
            pv = jnp.dot(vt_chunk[vsel[h] * KV_W:(vsel[h] + 1) * KV_W, :], p_scr[slot, h],
                         preferred_element_type=F32)
            acc_scr[h] = alpha[h] * acc_scr[h] + pv

    def softmax(slot, stats):
        new_stats, alpha = [], []
        for h in heads:
            sx = s_scr[slot, h]
            m_new = jnp.maximum(stats[h], jnp.max(sx, axis=0, keepdims=True))
            p_scr[slot, h] = jnp.exp2(sx - m_new).astype(BF16)
            alpha.append(jnp.exp2(stats[h] - m_new))
            new_stats.append(m_new)
        return tuple(new_stats), tuple(alpha)

    def step(c, parity, stats, alpha, has_prev, has_next):
        if has_next:
            put_scores(c + 1, 1 - parity)
        if has_prev:
            add_values(c - 1, 1 - parity, alpha)
        return softmax(parity, stats)

    first = -1 if has_ctx else 0
    last = nchunks - 1
    stats = tuple(jnp.full((1, tq), -1e30, F32) for _ in heads)
    alpha = tuple(jnp.zeros((1, tq), F32) for _ in heads)
    acc_scr[...] = jnp.zeros_like(acc_scr)
    put_scores(first, first % 2)
    c = first
    while c <= min(0, last):
        stats, alpha = step(c, c % 2, stats, alpha, c > first, c < last)
        c += 1
    npairs = max(last - c, 0) // 2

    def body(i, carry):
        stats, alpha = carry
        c0 = c + 2 * i
        stats, alpha = step(c0, c % 2, stats, alpha, True, True)
        return step(c0 + 1, (c + 1) % 2, stats, alpha, True, True)

    if npairs > 0:
        stats, alpha = lax.fori_loop(0, npairs, body, (stats, alpha))
        c += 2 * npairs
    while c <= last:
        stats, alpha = step(c, c % 2, stats, alpha, c > first, c < last)
        c += 1
    add_values(last, last % 2, alpha)
    res = []
    for h in heads:
        acc = acc_scr[h]
        res.append((acc / acc[den_row[h]:den_row[h] + 1, :]).T)
    blocks = [jnp.where(lower, res[2 * j], res[2 * j + 1]) for j in range(ATTN_W // LANES)]
    o_ref[...] = jnp.concatenate(blocks, axis=1).astype(BF16)


def _attention(q, k2, vt, k2c, vtc, batch, seq_len):
    ntok = q.shape[0]
    nq = seq_len // TOK_TILE
    has_ctx = k2c is not None
    in_specs = [
        pl.BlockSpec((TOK_TILE, ATTN_W), lambda b, i: (b * nq + i, 0)),
        pl.BlockSpec((nq, TOK_TILE, 2 * LANES), lambda b, i: (b, 0, 0)),
        pl.BlockSpec((nq, VT_ROWS, TOK_TILE), lambda b, i: (b, 0, 0)),
    ]
    args = [q, k2, vt]
    if has_ctx:
        in_specs += [pl.BlockSpec((1,) + k2c.shape[1:], lambda b, i: (b, 0, 0)),
                     pl.BlockSpec((1,) + vtc.shape[1:], lambda b, i: (b, 0, 0))]
        args += [k2c, vtc]
    return pl.pallas_call(
        functools.partial(_attn_kernel, has_ctx=has_ctx),
        grid=(batch, nq),
        in_specs=in_specs,
        out_specs=pl.BlockSpec((TOK_TILE, ATTN_W), lambda b, i: (b * nq + i, 0)),
        out_shape=jax.ShapeDtypeStruct((ntok, ATTN_W), BF16),
        scratch_shapes=[pltpu.VMEM((2, N_Q_HEADS, TOK_TILE, TOK_TILE), F32),
                        pltpu.VMEM((2, N_Q_HEADS, TOK_TILE, TOK_TILE), BF16),
                        pltpu.VMEM((N_Q_HEADS, KV_W, TOK_TILE), F32)],
        compiler_params=pltpu.CompilerParams(dimension_semantics=("arbitrary", "arbitrary"),
                                             vmem_limit_bytes=VMEM_LIMIT),
        name="attention",
    )(*args)


def _block_diag(x):
    seg = lax.broadcasted_iota(jnp.int32, x.shape, 1) // HEAD
    return jnp.concatenate([jnp.where(seg == s, x, jnp.zeros_like(x)) for s in range(SEGS)], axis=0)


def _dn_kernel(*refs, fwd, nblk, has_state, want_state, prepared):
    if prepared:
        qkv_ref, ab_ref, ones_ref, esel_ref = refs[:4]
        pos = 4
    else:
        x_ref, xp_ref, xn_ref, ab_ref, cw_ref, ones_ref, esel_ref = refs[:7]
        pos = 7
    if has_state:
        s0_ref = refs[pos]
        pos += 1
    o_ref = refs[pos]
    pos += 1
    if not prepared:
        qkv_out_ref = refs[pos]
        pos += 1
    if want_state:
        sf_ref = refs[pos]
        pos += 1
    s_scr = refs[pos]

    tb = TOK_TILE
    ngroups = DN_W // GROUP_W
    blk = pl.program_id(1)
    rb = blk if fwd else nblk - 1 - blk

    @pl.when(blk == 0)
    def _():
        if has_state:
            s_scr[...] = s0_ref[0]
        else:
            s_scr[...] = jnp.zeros_like(s_scr)

    ones_bd = ones_ref[...]
    if prepared:
        q = qkv_ref[:, 0:DN_W]
        k = qkv_ref[:, DN_W:2 * DN_W]
        v = qkv_ref[:, 2 * DN_W:3 * DN_W]
    else:
        x = x_ref[...]
        rows = lax.broadcasted_iota(jnp.int32, (tb, 1), 0)
        prev = jnp.where(rb > 0, xp_ref[7:8, :], 0.0)
        nxt = jnp.where(rb < nblk - 1, xn_ref[0:1, :], 0.0)
        xm1 = jnp.where(rows == 0, prev, pltpu.roll(x, 1, 0))
        xp1 = jnp.where(rows == tb - 1, nxt, pltpu.roll(x, tb - 1, 0))
        y = xm1 * cw_ref[0:1, :] + x * cw_ref[1:2, :] + xp1 * cw_ref[2:3, :]
        y = y * _sigmoid(y)
        q = y[:, 0:DN_W]
        k = y[:, DN_W:2 * DN_W]
        v = y[:, 2 * DN_W:3 * DN_W]
        q = q * lax.rsqrt(_seg_sumsq(q, ones_bd) + EPS) * (HEAD ** -0.5)
        k = k * lax.rsqrt(_seg_sumsq(k, ones_bd) + EPS)
        qkv_out_ref[:, 0:DN_W] = q
        qkv_out_ref[:, DN_W:2 * DN_W] = k
        qkv_out_ref[:, 2 * DN_W:3 * DN_W] = v

    ex = _sel_dot_left(ab_ref[...], esel_ref[...])
    la = ex[:, 0:DN_W]
    be = ex[:, DN_W:2 * DN_W]
    same_chunk = ones_bd.astype(F32)
    r_i = lax.broadcasted_iota(jnp.int32, (tb, tb), 0) % CHUNK
    c_i = lax.broadcasted_iota(jnp.int32, (tb, tb), 1) % CHUNK
    tri = jnp.where((c_i <= r_i) if fwd else (c_i >= r_i), ones_bd, jnp.zeros_like(ones_bd))
    g_col = _sel_dot_right(jnp.concatenate([tri] * 3, axis=1), la)
    t_i = lax.broadcasted_iota(jnp.int32, (tb, DN_W), 0) % CHUNK
    l_i = lax.broadcasted_iota(jnp.int32, (tb, DN_W), 1) % CHUNK
    g_row = _sel_dot_right(jnp.concatenate([ones_bd] * 3, axis=1),
                           jnp.where((t_i <= l_i) if fwd else (t_i >= l_i), la, 0.0))

    ii = lax.broadcasted_iota(jnp.int32, (CHUNK, GROUP_W), 0)
    jj = lax.broadcasted_iota(jnp.int32, (CHUNK, GROUP_W), 1) % HEAD
    incl = (jj <= ii) if fwd else (jj >= ii)
    strict = (jj < ii) if fwd else (jj > ii)
    eye = jnp.where(jj == ii, 1.0, 0.0)
    nt = (((1,), (1,)), ((), ()))
    tn = (((0,), (0,)), ((), ()))
    nchunk = tb // CHUNK

    order = [c if fwd else nchunk - 1 - c for c in range(nchunk)]
    pairs = [(slice(c * CHUNK, (c + 1) * CHUNK), slice(g * GROUP_W, (g + 1) * GROUP_W), g)
             for c in order for g in range(ngroups)]
    dot = functools.partial(jnp.dot, preferred_element_type=F32)

    dec, kq = [], []
    for rs, ls, _ in pairs:
        dec.append(jnp.exp(jnp.where(incl, g_col[rs, ls] - g_row[rs, ls], -1e30)))
        kc16 = k[rs, ls].astype(BF16)
        kq.append(lax.dot_general(jnp.concatenate([kc16, q[rs, ls].astype(BF16)], axis=0),
                                  _block_diag(kc16), nt, preferred_element_type=F32))

    neg_l = [-jnp.where(strict, be[rs, ls] * kqp[:CHUNK] * d, 0.0)
             for (rs, ls, _), kqp, d in zip(pairs, kq, dec)]
    acc = [eye + n for n in neg_l]
    power = list(neg_l)
    n_doublings = CHUNK.bit_length() - 3
    assert 2 ** (n_doublings + 2) == CHUNK
    for it in range(n_doublings + 1):
        for p in range(len(pairs)):
            wmat = _block_diag(power[p].astype(BF16))
            if it == 0:
                power[p] = dot(power[p].astype(BF16), wmat)
            elif it < n_doublings:
                r = dot(jnp.concatenate([acc[p], power[p]], axis=0).astype(BF16), wmat)
                acc[p] = acc[p] + r[:CHUNK]
                power[p] = r[CHUNK:]
            else:
                acc[p] = acc[p] + dot(acc[p].astype(BF16), wmat)

    xs = [_split2(a) for a in acc]
    ns = [_split2(n) for n in neg_l]
    nx1 = [dot(jnp.concatenate([nh, nl], axis=0), _block_diag(xh))
           for (nh, nl), (xh, _) in zip(ns, xs)]
    nx2 = [dot(nh, _block_diag(xl)) for (nh, _), (_, xl) in zip(ns, xs)]
    resid = [eye - a + (n1[:CHUNK] + n1[CHUNK:] + n2) for a, n1, n2 in zip(acc, nx1, nx2)]
    acc = [a + dot(xh, _block_diag(rsd.astype(BF16))) for a, (xh, _), rsd in zip(acc, xs, resid)]

    eg_c = [jnp.exp(g_col[rs, ls]) for rs, ls, _ in pairs]
    wu = []
    for (rs, ls, _), a, eg in zip(pairs, acc, eg_c):
        rw = (be[rs, ls] * eg * k[rs, ls]).astype(BF16)
        ru = (be[rs, ls] * v[rs, ls]).astype(BF16)
        wu.append(dot(a.astype(BF16), jnp.concatenate([_block_diag(rw), _block_diag(ru)], axis=1)))

    lhs, qk_d, k_dec, eg_last = [], [], [], []
    for (rs, ls, _), kqp, d, eg, wup in zip(pairs, kq, dec, eg_c, wu):
        gc = g_col[rs, ls]
        g_last = gc[CHUNK - 1:CHUNK, :] if fwd else gc[0:1, :]
        lhs.append(jnp.concatenate([wup[:, :GROUP_W], q[rs, ls] * eg], axis=0).astype(BF16))
        qk_d.append((kqp[CHUNK:] * d).astype(BF16))
        k_dec.append((k[rs, ls] * jnp.exp(g_last - gc)).astype(BF16))
        eg_last.append(jnp.exp(g_last))

    for step in range(nchunk):
        ps = range(step * ngroups, (step + 1) * ngroups)
        s_bd = [s_scr[pairs[p][2]] for p in ps]
        r = [dot(lhs[p], s.astype(BF16)) for p, s in zip(ps, s_bd)]
        u16 = [(wu[p][:, GROUP_W:] - rp[:CHUNK]).astype(BF16) for p, rp in zip(ps, r)]
        o_c = [rp[CHUNK:] + dot(qk_d[p], _block_diag(u)) for p, rp, u in zip(ps, r, u16)]
        upd = [lax.dot_general(k_dec[p], u, tn, preferred_element_type=F32) for p, u in zip(ps, u16)]
        for p, s, o, up in zip(ps, s_bd, o_c, upd):
            rs, ls, g = pairs[p]
            s_scr[g] = s * eg_last[p] + same_chunk * up
            o_ref[rs, ls] = o

    if want_state:
        @pl.when(blk == nblk - 1)
        def _():
            for g in range(ngroups):
                for s in range(SEGS):
                    rows = s_scr[g, s * HEAD:(s + 1) * HEAD, :]
                    sf_ref[0, g * SEGS + s] = rows[:, s * HEAD:(s + 1) * HEAD]


def _deltanet(src, ab, conv_w, ones_bd, esel, s0_bd, batch, seq_len, fwd, want_state):
    ntok = src.shape[0]
    nblk = seq_len // TOK_TILE
    has_state = s0_bd is not None
    prepared = conv_w is None
    ngroups = DN_W // GROUP_W
    sub = TOK_TILE // 8

    def blkpos(b, i):
        return b * nblk + (i if fwd else nblk - 1 - i)

    tok_spec = lambda width: pl.BlockSpec((TOK_TILE, width), lambda b, i: (blkpos(b, i), 0))
    fixed = lambda shape: pl.BlockSpec(shape, lambda b, i: (0, 0))
    if prepared:
        in_specs = [tok_spec(3 * DN_W), tok_spec(LANES), fixed((GROUP_W, GROUP_W)),
                    fixed((3 * LANES, 2 * DN_W))]
        args = [src, ab, ones_bd, esel]
    else:
        in_specs = [
            tok_spec(3 * DN_W),
            pl.BlockSpec((8, 3 * DN_W), lambda b, i: (jnp.maximum(blkpos(b, i) * sub - 1, 0), 0)),
            pl.BlockSpec((8, 3 * DN_W),
                         lambda b, i: (jnp.minimum((blkpos(b, i) + 1) * sub, ntok // 8 - 1), 0)),
            tok_spec(LANES), fixed((3, 3 * DN_W)), fixed((GROUP_W, GROUP_W)),
            fixed((3 * LANES, 2 * DN_W)),
        ]
        args = [src, src, src, ab, conv_w, ones_bd, esel]
    if has_state:
        in_specs.append(pl.BlockSpec((1, ngroups, GROUP_W, GROUP_W), lambda b, i: (b, 0, 0, 0)))
        args.append(s0_bd)
    out_specs = [tok_spec(DN_W)]
    out_shape = [jax.ShapeDtypeStruct((ntok, DN_W), F32)]
    if not prepared:
        out_specs.append(tok_spec(3 * DN_W))
        out_shape.append(jax.ShapeDtypeStruct((ntok, 3 * DN_W), F32))
    if want_state:
        out_specs.append(pl.BlockSpec((1, N_DN_HEADS, HEAD, HEAD), lambda b, i: (b, 0, 0, 0)))
        out_shape.append(jax.ShapeDtypeStruct((batch, N_DN_HEADS, HEAD, HEAD), F32))
    return pl.pallas_call(
        functools.partial(_dn_kernel, fwd=fwd, nblk=nblk, has_state=has_state,
                          want_state=want_state, prepared=prepared),
        grid=(batch, nblk),
        in_specs=in_specs,
        out_specs=out_specs,
        out_shape=out_shape,
        scratch_shapes=[pltpu.VMEM((ngroups, GROUP_W, GROUP_W), F32)],
        compiler_params=pltpu.CompilerParams(dimension_semantics=("arbitrary", "arbitrary"),
                                             vmem_limit_bytes=VMEM_LIMIT),
        name="deltanet_fwd" if fwd else "deltanet_bwd",
    )(*args)


def _post_kernel(*refs, final):
    (x_ref, oa_ref, of_ref, ob_ref, gout_ref, gates_ref, mod_ref, wpa_ref, wpd_ref, wout_ref,
     dng_ref, ones_ref, n2_ref, w1_ref, b1_ref, w2_ref, b2_ref) = refs[:17]
    if final:
        fn_ref, o_ref = refs[17:19]
    else:
        o_ref = refs[17]
    od = of_ref[...] + ob_ref[...]
    od = od * lax.rsqrt(_seg_sumsq(od, ones_ref[...]) * (1.0 / HEAD) + EPS) * dng_ref[...]
    gout = gout_ref[...]
    od = od * (gout * _sigmoid(gout))
    pa = jnp.dot(oa_ref[...], wpa_ref[...], preferred_element_type=F32)
    pd = jnp.dot(od.astype(BF16), wpd_ref[...], preferred_element_type=F32)
    merged = gates_ref[:, 0:D_MODEL] * pa + gates_ref[:, D_MODEL:2 * D_MODEL] * pd
    gate1 = mod_ref[0, :, 2 * D_MODEL:3 * D_MODEL]
    x = x_ref[...] + gate1 * jnp.dot(merged.astype(BF16), wout_ref[...],
                                     preferred_element_type=F32)

    shift2 = mod_ref[0, :, 3 * D_MODEL:4 * D_MODEL]
    scale2 = mod_ref[0, :, 4 * D_MODEL:5 * D_MODEL]
    gate2 = mod_ref[0, :, 5 * D_MODEL:6 * D_MODEL]
    h = _rms_rows(x) * n2_ref[...]
    h = h * (1.0 + scale2) + shift2
    a = jnp.dot(h.astype(BF16), w1_ref[...], preferred_element_type=F32) + b1_ref[...]
    a = jnp.square(jnp.maximum(a, 0.0))
    ff = jnp.dot(a.astype(BF16), w2_ref[...], preferred_element_type=F32) + b2_ref[...]
    out = x + gate2 * ff
    if final:
        out = _rms_rows(out) * fn_ref[...]
    o_ref[...] = out


def _post(x2d, oa, o_f, o_b, dcat, gates, mod, tiles_per_mod, wpa, wpd, wout, dng, ones_bd, n2, w1,
          b1, w2, b2, final_norm):
    ntok = x2d.shape[0]
    final = final_norm is not None
    row = lambda i: (i, 0)
    fixed = lambda shape: pl.BlockSpec(shape, lambda i: (0, 0), pipeline_mode=pl.Buffered(1))
    in_specs = [
        pl.BlockSpec((TOK_TILE, D_MODEL), row),
        pl.BlockSpec((TOK_TILE, ATTN_W), row),
        pl.BlockSpec((TOK_TILE, DN_W), row),
        pl.BlockSpec((TOK_TILE, DN_W), row),
        pl.BlockSpec((TOK_TILE, DN_W), lambda i: (i, 3)),
        pl.BlockSpec((TOK_TILE, 2 * D_MODEL), row),
        pl.BlockSpec((1, 1, 6 * D_MODEL), lambda i: (i // tiles_per_mod, 0, 0)),
        fixed((ATTN_W, D_MODEL)),
        fixed((DN_W, D_MODEL)),
        fixed((D_MODEL, D_MODEL)),
        fixed((1, DN_W)),
        fixed((GROUP_W, GROUP_W)),
        fixed((1, D_MODEL)),
        fixed((D_MODEL, D_FF)),
        fixed((1, D_FF)),
        fixed((D_FF, D_MODEL)),
        fixed((1, D_MODEL)),
    ]
    args = [x2d, oa, o_f, o_b, dcat, gates, mod, wpa, wpd, wout, dng, ones_bd, n2, w1, b1, w2, b2]
    if final:
        in_specs.append(fixed((1, D_MODEL)))
        args.append(final_norm)
    return pl.pallas_call(
        functools.partial(_post_kernel, final=final),
        grid=(ntok // TOK_TILE,),
        in_specs=in_specs,
        out_specs=pl.BlockSpec((TOK_TILE, D_MODEL), row),
        out_shape=jax.ShapeDtypeStruct((ntok, D_MODEL), F32),
        compiler_params=pltpu.CompilerParams(dimension_semantics=("arbitrary",),
                                             vmem_limit_bytes=VMEM_LIMIT),
        name="merge_mlp",
    )(*args)


def _rope_tables(n_tokens):
    freqs = HEAD // 4
    t = jnp.arange(n_tokens)
    row = (t // GRID_W).astype(F32)
    col = (t % GRID_W).astype(F32)
    inv = ROPE_THETA ** (-jnp.arange(freqs, dtype=F32) / freqs)
    ang = jnp.concatenate([row[:, None] * inv, col[:, None] * inv], axis=-1)
    cos, sin = jnp.cos(ang), jnp.sin(ang)
    cos_h = jnp.concatenate([cos, cos], axis=-1)
    sin_h = jnp.concatenate([-sin, sin], axis=-1)
    reps = LANES // HEAD
    return jnp.tile(cos_h, (1, reps)), jnp.tile(sin_h, (1, reps))


def _pack_w_in(w):
    pad = jnp.zeros((D_MODEL, LANES - 4 * N_DN_HEADS), w.dtype)
    return jnp.concatenate([w[:, :C_AB], w[:, C_AB:C_AB + 4 * N_DN_HEADS], pad,
                            w[:, C_AB + 4 * N_DN_HEADS:]], axis=1).astype(BF16)


def _pad_lanes(v):
    v = v.reshape(1, -1)
    return jnp.pad(v, ((0, 0), (0, LANES - v.shape[1])))


def _expand_selector(direction):
    src = jnp.arange(LANES)[:, None]
    dst = jnp.arange(2 * DN_W)[None, :]
    head = (dst % DN_W) // HEAD
    want = jnp.where(dst < DN_W, direction * N_DN_HEADS + head,
                     2 * N_DN_HEADS + direction * N_DN_HEADS + head)
    return jnp.tile((src == want).astype(BF16), (3, 1))


def _kv_head_orders(k, v):
    k2 = jnp.concatenate([k[:, 0], k[:, 1], k[:, 1], k[:, 0]], axis=-1)
    ones = jnp.ones_like(v[:, 0])
    vt = jnp.concatenate([v[:, 0], ones, ones, v[:, 0], v[:, 1], ones, ones, v[:, 1]], axis=-1)
    return k2.astype(BF16), vt.transpose(0, 2, 1).astype(BF16)


def _state_to_block_diag(s):
    b = s.shape[0]
    s = s.reshape(b, DN_W // GROUP_W, SEGS, HEAD, HEAD).astype(F32)
    eye = jnp.eye(SEGS, dtype=F32)
    bd = s[:, :, :, :, None, :] * eye[None, None, :, None, :, None]
    return bd.reshape(b, DN_W // GROUP_W, GROUP_W, GROUP_W)


def kernel(x_prompt, x_sample, c, cache_k, cache_v, state_delta, c_ctx, w_mod, b_mod, norm1, norm2,
           w_in, conv_w, q_gain, k_gain, a_log, dt_bias, dn_gain, w_pa, w_pd, w_out, w1, b1, w2, b2,
           final_norm):
    depth = w_in.shape[0]
    batch, seq, _ = x_prompt.shape
    dec_batch, dec_seq, _ = x_sample.shape
    assert seq == TOK_TILE and dec_seq % TOK_TILE == 0

    mod_rows = 8
    cvec = jnp.concatenate([c, c_ctx[None, :],
                            jnp.zeros((mod_rows - dec_batch - 1, D_MODEL), F32)], axis=0)
    mod_all = _modulation(cvec, w_mod.astype(BF16), b_mod)

    ones_bd = (jnp.arange(GROUP_W)[:, None] // HEAD == jnp.arange(GROUP_W)[None, :] // HEAD
               ).astype(BF16)
    esel = [_expand_selector(0), _expand_selector(1)]
    rope_tabs = _rope_tables(dec_seq)

    xp = x_prompt.reshape(batch * seq, D_MODEL)
    xs = x_sample.reshape(dec_batch * dec_seq, D_MODEL)
    ks_out, vs_out, st_out = [], [], []
    for l in range(depth):
        w_cat = _pack_w_in(w_in[l])
        wvt = w_in[l][:, ATTN_W + KV_W:ATTN_W + 2 * KV_W].T.astype(BF16)
        n1 =norm1[l].reshape(1, D_MODEL)
        n2 = norm2[l].reshape(1, D_MODEL)
        qg = jnp.tile(q_gain[l], N_Q_HEADS).reshape(1, ATTN_W)
        kg = jnp.tile(k_gain[l], N_KV_HEADS).reshape(1, KV_W)
        dng = jnp.tile(dn_gain[l], N_DN_HEADS).reshape(1, DN_W)
        alog = _pad_lanes(a_log[l])
        dtb = _pad_lanes(dt_bias[l])
        wpa, wpd, wout = w_pa[l].astype(BF16), w_pd[l].astype(BF16), w_out[l].astype(BF16)
        w1b, w2b = w1[l].astype(BF16), w2[l].astype(BF16)
        b1r, b2r = b1[l].reshape(1, D_FF), b2[l].reshape(1, D_MODEL)
        last = l == depth - 1
        fin = final_norm.reshape(1, D_MODEL) if last else None
        mod_lat = mod_all[l, :dec_batch].reshape(dec_batch, 1, 6 * D_MODEL)
        mod_ctx = mod_all[l, dec_batch:dec_batch + 1].reshape(1, 1, 6 * D_MODEL)

        tpm = batch * seq // TOK_TILE
        q, k2, vt, dcat, ab, gates, k_c, v_c = _inproj(xp, mod_ctx, tpm, n1, w_cat, wvt, qg, kg,
                                                       alog, dtb, ones_bd, None, True, seq)
        oa = _attention(q, k2, vt, None, None, batch, seq)
        o_f, qkv, s_f = _deltanet(dcat, ab, conv_w[l], ones_bd, esel[0], None, batch, seq, True, True)
        o_b, s_b = _deltanet(qkv, ab, None, ones_bd, esel[1], None, batch, seq, False, True)
        xp = _post(xp, oa, o_f, o_b, dcat, gates, mod_ctx, tpm, wpa, wpd, wout, dng, ones_bd, n2,
                   w1b, b1r, w2b, b2r, fin)
        ks_out.append(k_c)
        vs_out.append(v_c)
        st_out.append(jnp.stack([s_f, s_b], axis=1))

        tpm = dec_seq // TOK_TILE
        k2c, vtc = _kv_head_orders(cache_k[:, l], cache_v[:, l])
        q, k2, vt, dcat, ab, gates = _inproj(xs, mod_lat, tpm, n1, w_cat, wvt, qg, kg, alog, dtb,
                                             ones_bd, rope_tabs, False, dec_seq)
        oa = _attention(q, k2, vt, k2c, vtc, dec_batch, dec_seq)
        s0f = _state_to_block_diag(state_delta[:, l, 0])
        s0b = _state_to_block_diag(state_delta[:, l, 1])
        o_f, qkv = _deltanet(dcat, ab, conv_w[l], ones_bd, esel[0], s0f, dec_batch, dec_seq, True,
                             False)
        (o_b,) = _deltanet(qkv, ab, None, ones_bd, esel[1], s0b, dec_batch, dec_seq, False, False)
        xs = _post(xs, oa, o_f, o_b, dcat, gates, mod_lat, tpm, wpa, wpd, wout, dng, ones_bd, n2,
                   w1b, b1r, w2b, b2r, fin)

    y_prompt = xp.reshape(batch, seq, D_MODEL)
    y_sample = xs.reshape(dec_batch, dec_seq, D_MODEL)
    return (y_prompt, y_sample, jnp.stack(ks_out, axis=1), jnp.stack(vs_out, axis=1),
            jnp.stack(st_out, axis=1))
```

```python
import functools

import jax
import jax.numpy as jnp
from jax import lax
from jax.experimental import pallas as pl
from jax.experimental.pallas import tpu as pltpu

F32 = jnp.float32
BF16 = jnp.bfloat16

D_MODEL = 1024
HEAD = 64
N_Q_HEADS = 8
N_KV_HEADS = 2
N_DN_HEADS = 8
ATTN_W = N_Q_HEADS * HEAD
KV_W = N_KV_HEADS * HEAD
DN_W = N_DN_HEADS * HEAD
CHUNK = 64
GRID_W = 64
D_FF = 4 * D_MODEL
ROPE_THETA = 10000.0
EPS = 1e-6
LOG2_E = 1.4426950408889634

LANES = 128
GROUP_W = 256
SEGS = GROUP_W // HEAD
TOK_TILE = 256
V_ROWS = HEAD + 16
VT_ROWS = N_KV_HEADS * V_ROWS
ATTN_CHUNK_TILES = 2
VMEM_LIMIT = 56 * 1024 * 1024

C_A = 0
C_D = ATTN_W + 2 * KV_W
C_AB = C_D + 4 * DN_W
C_G = C_AB + LANES
C_END = C_G + 2 * D_MODEL


def _sigmoid(x):
    return 1.0 / (1.0 + jnp.exp(-x))


def _bdot(a, b):
    return jnp.dot(a.astype(BF16), b.astype(BF16), preferred_element_type=F32)


def _split2(x):
    hi = x.astype(BF16)
    return hi, (x - hi.astype(F32)).astype(BF16)


def _split3(x):
    hi = x.astype(BF16)
    r = x - hi.astype(F32)
    mid = r.astype(BF16)
    return hi, mid, (r - mid.astype(F32)).astype(BF16)


def _sel_dot_left(x, sel3):
    return jnp.dot(jnp.concatenate(_split3(x), axis=1), sel3, preferred_element_type=F32)


def _sel_dot_right(sel3, x):
    return jnp.dot(sel3, jnp.concatenate(_split3(x), axis=0), preferred_element_type=F32)


def _seg_sumsq(x, ones_bd):
    x2 = x * x
    hi = x2.astype(BF16)
    lo = (x2 - hi.astype(F32)).astype(BF16)
    width = x.shape[1]
    if width == LANES:
        m = ones_bd[:LANES, :LANES]
        return (jnp.dot(hi, m, preferred_element_type=F32)
                + jnp.dot(lo, m, preferred_element_type=F32))
    outs = []
    for g in range(width // GROUP_W):
        sl = slice(g * GROUP_W, (g + 1) * GROUP_W)
        outs.append(jnp.dot(hi[:, sl], ones_bd, preferred_element_type=F32)
                    + jnp.dot(lo[:, sl], ones_bd, preferred_element_type=F32))
    return jnp.concatenate(outs, axis=1)


def _rms_rows(x):
    return x * lax.rsqrt(jnp.mean(x * x, axis=-1, keepdims=True) + EPS)


def _mod_kernel(c_ref, w_ref, b_ref, o_ref):
    c = c_ref[...]
    s = c * _sigmoid(c)
    o_ref[0] = jnp.dot(s.astype(BF16), w_ref[0], preferred_element_type=F32) + b_ref[0]


def _modulation(cvec, w_mod_bf, b_mod):
    depth = w_mod_bf.shape[0]
    rows = cvec.shape[0]
    nblk = 6 * D_MODEL // D_MODEL
    return pl.pallas_call(
        _mod_kernel,
        grid=(depth, nblk),
        in_specs=[
            pl.BlockSpec((rows, D_MODEL), lambda l, j: (0, 0)),
            pl.BlockSpec((1, D_MODEL, D_MODEL), lambda l, j: (l, 0, j)),
            pl.BlockSpec((1, 1, D_MODEL), lambda l, j: (l, 0, j)),
        ],
        out_specs=pl.BlockSpec((1, rows, D_MODEL), lambda l, j: (l, 0, j)),
        out_shape=jax.ShapeDtypeStruct((depth, rows, 6 * D_MODEL), F32),
        name="adaln_mod",
    )(cvec, w_mod_bf, b_mod.reshape(depth, 1, 6 * D_MODEL))


def _rope(x, cos, sin_signed):
    width = x.shape[1]
    lane = lax.broadcasted_iota(jnp.int32, x.shape, 1)
    first_half = (lane % HEAD) < (HEAD // 2)
    rot = jnp.where(first_half, pltpu.roll(x, width - HEAD // 2, 1), pltpu.roll(x, HEAD // 2, 1))
    return x * cos + rot * sin_signed


def _inproj_kernel(*refs, rope, cache_out):
    (x_ref, mod_ref, n1_ref, w_ref, wvt_ref, qg_ref, kg_ref, alog_ref, dtb_ref,
     ones_ref) = refs[:10]
    pos = 10
    if rope:
        cos_ref, sin_ref = refs[pos:pos + 2]
        pos += 2
    q_ref, k2_ref, vt_ref, d_ref, ab_ref, g_ref = refs[pos:pos + 6]
    pos += 6
    if cache_out:
        kc_ref, vc_ref = refs[pos:pos + 2]

    shift1 = mod_ref[0, :, 0:D_MODEL]
    scale1 = mod_ref[0, :, D_MODEL:2 * D_MODEL]
    h = _rms_rows(x_ref[...]) * n1_ref[...]
    h = h * (1.0 + scale1) + shift1
    h16 = h.astype(BF16)
    y = jnp.dot(h16, w_ref[...], preferred_element_type=F32)
    ones_bd = ones_ref[...]

    vt = lax.dot_general(wvt_ref[...], h16, (((1,), (1,)), ((), ())),
                         preferred_element_type=F32).astype(BF16)
    ones = jnp.ones((V_ROWS - HEAD, vt.shape[1]), BF16)
    for kv in range(N_KV_HEADS):
        vt_ref[0, kv * V_ROWS:kv * V_ROWS + HEAD, :] = vt[kv * HEAD:(kv + 1) * HEAD]
        vt_ref[0, kv * V_ROWS + HEAD:(kv + 1) * V_ROWS, :] = ones

    qa = y[:, C_A:C_A + ATTN_W]
    qn = qa * lax.rsqrt(_seg_sumsq(qa, ones_bd) * (1.0 / HEAD) + EPS) * qg_ref[...]
    ka = y[:, C_A + ATTN_W:C_A + ATTN_W + KV_W]
    kn = ka * lax.rsqrt(_seg_sumsq(ka, ones_bd) * (1.0 / HEAD) + EPS) * kg_ref[...]
    va = y[:, C_A + ATTN_W + KV_W:C_D]
    if cache_out:
        kc_ref[0, 0] = kn[:, :HEAD]
        kc_ref[0, 1] = kn[:, HEAD:]
        vc_ref[0, 0] = va[:, :HEAD]
        vc_ref[0, 1] = va[:, HEAD:]
    if rope:
        cos = cos_ref[...]
        sin = sin_ref[...]
        qn = _rope(qn, jnp.concatenate([cos] * (ATTN_W // LANES), axis=1),
                   jnp.concatenate([sin] * (ATTN_W // LANES), axis=1))
        kn = _rope(kn, cos, sin)
    q_ref[...] = (qn * (HEAD ** -0.5 * LOG2_E)).astype(BF16)
    k2_ref[0, :, 0:LANES] = kn.astype(BF16)
    k2_ref[0, :, LANES:2 * LANES] = pltpu.roll(kn, HEAD, 1).astype(BF16)

    d_ref[...] = y[:, C_D:C_AB]

    z = y[:, C_AB:C_G]
    zb = z + dtb_ref[...]
    softplus = jnp.maximum(zb, 0.0) + jnp.log(1.0 + jnp.exp(-jnp.abs(zb)))
    log_a = -jnp.exp(alog_ref[...]) * softplus
    lane = lax.broadcasted_iota(jnp.int32, z.shape, 1)
    ab_ref[...] = jnp.where(lane < 2 * N_DN_HEADS, log_a,
                            jnp.where(lane < 4 * N_DN_HEADS, _sigmoid(z), 0.0))

    g_ref[...] = _sigmoid(y[:, C_G:C_END]).astype(BF16)


def _inproj(x2d, mod, tiles_per_mod, n1, w_cat, wvt, qg, kg, alog, dtb, ones_bd, rope_tabs,
            cache_out, seq_len):
    ntok = x2d.shape[0]
    ntiles = ntok // TOK_TILE
    rope = rope_tabs is not None
    const = lambda i: (0, 0)
    row = lambda i: (i, 0)
    in_specs = [
        pl.BlockSpec((TOK_TILE, D_MODEL), row),
        pl.BlockSpec((1, 1, 6 * D_MODEL), lambda i: (i // tiles_per_mod, 0, 0)),
        pl.BlockSpec((1, D_MODEL), const),
        pl.BlockSpec((D_MODEL, C_END), const),
        pl.BlockSpec((KV_W, D_MODEL), const),
        pl.BlockSpec((1, ATTN_W), const),
        pl.BlockSpec((1, KV_W), const),
        pl.BlockSpec((1, LANES), const),
        pl.BlockSpec((1, LANES), const),
        pl.BlockSpec((GROUP_W, GROUP_W), const),
    ]
    args = [x2d, mod, n1, w_cat, wvt, qg, kg, alog, dtb, ones_bd]
    if rope:
        tiles_per_seq = seq_len // TOK_TILE
        in_specs += [pl.BlockSpec((TOK_TILE, LANES), lambda i: (i % tiles_per_seq, 0))] * 2
        args += list(rope_tabs)
    out_specs = [
        pl.BlockSpec((TOK_TILE, ATTN_W), row),
        pl.BlockSpec((1, TOK_TILE, 2 * LANES), lambda i: (i, 0, 0)),
        pl.BlockSpec((1, VT_ROWS, TOK_TILE), lambda i: (i, 0, 0)),
        pl.BlockSpec((TOK_TILE, 4 * DN_W), row),
        pl.BlockSpec((TOK_TILE, LANES), row),
        pl.BlockSpec((TOK_TILE, 2 * D_MODEL), row),
    ]
    out_shape = [
        jax.ShapeDtypeStruct((ntok, ATTN_W), BF16),
        jax.ShapeDtypeStruct((ntiles, TOK_TILE, 2 * LANES), BF16),
        jax.ShapeDtypeStruct((ntiles, VT_ROWS, TOK_TILE), BF16),
        jax.ShapeDtypeStruct((ntok, 4 * DN_W), F32),
        jax.ShapeDtypeStruct((ntok, LANES), F32),
        jax.ShapeDtypeStruct((ntok, 2 * D_MODEL), BF16),
    ]
    if cache_out:
        assert seq_len == TOK_TILE
        out_specs += [pl.BlockSpec((1, N_KV_HEADS, TOK_TILE, HEAD), lambda i: (i, 0, 0, 0))] * 2
        out_shape += [jax.ShapeDtypeStruct((ntiles, N_KV_HEADS, TOK_TILE, HEAD), F32)] * 2
    return pl.pallas_call(
        functools.partial(_inproj_kernel, rope=rope, cache_out=cache_out),
        grid=(ntiles,),
        in_specs=in_specs,
        out_specs=out_specs,
        out_shape=out_shape,
        compiler_params=pltpu.CompilerParams(dimension_semantics=("arbitrary",),
                                             vmem_limit_bytes=VMEM_LIMIT),
        name="inproj",
    )(*args)


def _attn_kernel(*refs, has_ctx):
    if has_ctx:
        q_ref, k2_ref, vt_ref, k2c_ref, vtc_ref, o_ref, s_scr, p_scr, acc_scr = refs
    else:
        q_ref, k2_ref, vt_ref, o_ref, s_scr, p_scr, acc_scr = refs
    tq = q_ref.shape[0]
    lane = lax.broadcasted_iota(jnp.int32, (1, LANES), 1)
    lower = lane < HEAD
    nt = (((1,), (1,)), ((), ()))
    qh, sel, vrow = [], [], []
    for head in range(N_Q_HEADS):
        j, half = divmod(head, 2)
        kv = head // (N_Q_HEADS // N_KV_HEADS)
        qb = q_ref[:, j * LANES:(j + 1) * LANES]
        qh.append(jnp.where(lower if half == 0 else jnp.logical_not(lower), qb, jnp.zeros_like(qb)))
        sel.append(0 if half == kv else 1)
        vrow.append(kv * V_ROWS)

    heads = range(N_Q_HEADS)
    tiles = s_scr.shape[2] // TOK_TILE
    nchunks = k2_ref.shape[0] // tiles

    def is_ctx(c):
        return has_ctx and isinstance(c, int) and c < 0

    def key_tiles(c):
        if is_ctx(c):
            return [(k2c_ref[0], vtc_ref[0])]
        return [(k2_ref[c * tiles + t], vt_ref[c * tiles + t]) for t in range(tiles)]

    def put_scores(c, slot):
        k_chunk = jnp.concatenate([kt for kt, _ in key_tiles(c)], axis=0)
        for h in heads:
            s_scr[slot, h, 0:k_chunk.shape[0]] = lax.dot_general(
                k_chunk[:, sel[h] * LANES:(sel[h] + 1) * LANES], qh[h], nt,
                preferred_element_type=F32)

    def add_values(c, slot, alpha):
        vts = [vt for _, vt in key_tiles(c)]
        for h in heads:
            pv = alpha[h] * acc_scr[h]
            for t, vt_tile in enumerate(vts):
                pv = pv + jnp.dot(vt_tile[vrow[h]:vrow[h] + V_ROWS, :],
                                  p_scr[slot, h, t * TOK_TILE:(t + 1) * TOK_TILE],
                                  preferred_element_type=F32)
            acc_scr[h] = pv

    def softmax(c, slot, stats):
        rows = k2c_ref.shape[1] if is_ctx(c) else tiles * TOK_TILE
        new_stats, alpha = [], []
        for h in heads:
            sx = s_scr[slot, h, 0:rows]
            m_new = jnp.maximum(stats[h], jnp.max(sx, axis=0, keepdims=True))
            p_scr[slot, h, 0:rows] = jnp.exp2(sx - m_new).astype(BF16)
            alpha.append(jnp.exp2(stats[h] - m_new))
            new_stats.append(m_new)
        return tuple(new_stats), tuple(alpha)

    def step(c, parity, stats, alpha, has_prev, has_next):
        if has_next:
            put_scores(c + 1, 1 - parity)
        if has_prev:
            add_values(c - 1, 1 - parity, alpha)
        return softmax(c, parity, stats)

    first = -1 if has_ctx else 0
    last = nchunks - 1
    stats = tuple(jnp.full((1, tq), -1e30, F32) for _ in heads)
    alpha = tuple(jnp.zeros((1, tq), F32) for _ in heads)
    acc_scr[...] = jnp.zeros_like(acc_scr)
    put_scores(first, first % 2)
    c = first
    while c <= min(0, last):
        stats, alpha = step(c, c % 2, stats, alpha, c > first, c < last)
        c += 1
    per_trip = 2
    ntrips = max(last - c, 0) // per_trip

    def body(i, carry):
        c0 = c + per_trip * i
        for j in range(per_trip):
            carry = step(c0 + j, (c + j) % 2, *carry, True, True)
        return carry

    if ntrips > 0:
        stats, alpha = lax.fori_loop(0, ntrips, body, (stats, alpha))
        c += per_trip * ntrips
    while c <= last:
        stats, alpha = step(c, c % 2, stats, alpha, c > first, c < last)
        c += 1
    add_values(last, last % 2, alpha)
    res = []
    for h in heads:
        acc = acc_scr[h]
        res.append(acc[0:HEAD, :] / acc[HEAD:HEAD + 1, :])
    blocks = [jnp.concatenate(res[2 * j:2 * j + 2], axis=0).T for j in range(ATTN_W // LANES)]
    o_ref[...] = jnp.concatenate(blocks, axis=1).astype(BF16)


def _attention(q, k2, vt, k2c, vtc, batch, seq_len):
    ntok = q.shape[0]
    nq = seq_len // TOK_TILE
    has_ctx = k2c is not None
    chunk_keys = ATTN_CHUNK_TILES * TOK_TILE if nq % ATTN_CHUNK_TILES == 0 else TOK_TILE
    in_specs = [
        pl.BlockSpec((TOK_TILE, ATTN_W), lambda b, i: (b * nq + i, 0)),
        pl.BlockSpec((nq, TOK_TILE, 2 * LANES), lambda b, i: (b, 0, 0)),
        pl.BlockSpec((nq, VT_ROWS, TOK_TILE), lambda b, i: (b, 0, 0)),
    ]
    args = [q, k2, vt]
    if has_ctx:
        in_specs += [pl.BlockSpec((1,) + k2c.shape[1:], lambda b, i: (b, 0, 0)),
                     pl.BlockSpec((1,) + vtc.shape[1:], lambda b, i: (b, 0, 0))]
        args += [k2c, vtc]
    return pl.pallas_call(
        functools.partial(_attn_kernel, has_ctx=has_ctx),
        grid=(batch, nq),
        in_specs=in_specs,
        out_specs=pl.BlockSpec((TOK_TILE, ATTN_W), lambda b, i: (b * nq + i, 0)),
        out_shape=jax.ShapeDtypeStruct((ntok, ATTN_W), BF16),
        scratch_shapes=[pltpu.VMEM((2, N_Q_HEADS, chunk_keys, TOK_TILE), F32),
                        pltpu.VMEM((2, N_Q_HEADS, chunk_keys, TOK_TILE), BF16),
                        pltpu.VMEM((N_Q_HEADS, V_ROWS, TOK_TILE), F32)],
        compiler_params=pltpu.CompilerParams(dimension_semantics=("arbitrary", "arbitrary"),
                                             vmem_limit_bytes=VMEM_LIMIT),
        name="attention",
    )(*args)


def _block_diag(x):
    seg = lax.broadcasted_iota(jnp.int32, x.shape, 1) // HEAD
    return jnp.concatenate([jnp.where(seg == s, x, jnp.zeros_like(x)) for s in range(SEGS)], axis=0)


def _dn_kernel(*refs, fwd, nblk, has_state, want_state, prepared):
    if prepared:
        qkv_ref, ab_ref, ones_ref, esel_ref = refs[:4]
        pos = 4
    else:
        x_ref, xp_ref, xn_ref, ab_ref, cw_ref, ones_ref, esel_ref = refs[:7]
        pos = 7
    if has_state:
        s0_ref = refs[pos]
        pos += 1
    o_ref = refs[pos]
    pos += 1
    if not prepared:
        qkv_out_ref = refs[pos]
        pos += 1
    if want_state:
        sf_ref = refs[pos]
        pos += 1
    s_scr = refs[pos]

    tb = TOK_TILE
    ngroups = DN_W // GROUP_W
    blk = pl.program_id(1)
    rb = blk if fwd else nblk - 1 - blk

    @pl.when(blk == 0)
    def _():
        if has_state:
            s_scr[...] = s0_ref[0]
        else:
            s_scr[...] = jnp.zeros_like(s_scr)

    ones_bd = ones_ref[...]
    if prepared:
        q = qkv_ref[:, 0:DN_W]
        k = qkv_ref[:, DN_W:2 * DN_W]
        v = qkv_ref[:, 2 * DN_W:3 * DN_W]
    else:
        x = x_ref[...]
        rows = lax.broadcasted_iota(jnp.int32, (tb, 1), 0)
        prev = jnp.where(rb > 0, xp_ref[7:8, :], 0.0)
        nxt = jnp.where(rb < nblk - 1, xn_ref[0:1, :], 0.0)
        xm1 = jnp.where(rows == 0, prev, pltpu.roll(x, 1, 0))
        xp1 = jnp.where(rows == tb - 1, nxt, pltpu.roll(x, tb - 1, 0))
        y = xm1 * cw_ref[0:1, :] + x * cw_ref[1:2, :] + xp1 * cw_ref[2:3, :]
        y = y * _sigmoid(y)
        q = y[:, 0:DN_W]
        k = y[:, DN_W:2 * DN_W]
        v = y[:, 2 * DN_W:3 * DN_W]
        q = q * lax.rsqrt(_seg_sumsq(q, ones_bd) + EPS) * (HEAD ** -0.5)
        k = k * lax.rsqrt(_seg_sumsq(k, ones_bd) + EPS)
        qkv_out_ref[:, 0:DN_W] = q
        qkv_out_ref[:, DN_W:2 * DN_W] = k
        qkv_out_ref[:, 2 * DN_W:3 * DN_W] = v

    ex = _sel_dot_left(ab_ref[...], esel_ref[...])
    la = ex[:, 0:DN_W]
    be = ex[:, DN_W:2 * DN_W]
    same_chunk = ones_bd.astype(F32)
    r_i = lax.broadcasted_iota(jnp.int32, (tb, tb), 0) % CHUNK
    c_i = lax.broadcasted_iota(jnp.int32, (tb, tb), 1) % CHUNK
    tri = jnp.where((c_i <= r_i) if fwd else (c_i >= r_i), ones_bd, jnp.zeros_like(ones_bd))
    g_col = _sel_dot_right(jnp.concatenate([tri] * 3, axis=1), la)
    t_i = lax.broadcasted_iota(jnp.int32, (tb, DN_W), 0) % CHUNK
    l_i = lax.broadcasted_iota(jnp.int32, (tb, DN_W), 1) % CHUNK
    g_row = _sel_dot_right(jnp.concatenate([ones_bd] * 3, axis=1),
                           jnp.where((t_i <= l_i) if fwd else (t_i >= l_i), la, 0.0))

    ii = lax.broadcasted_iota(jnp.int32, (CHUNK, GROUP_W), 0)
    jj = lax.broadcasted_iota(jnp.int32, (CHUNK, GROUP_W), 1) % HEAD
    incl = (jj <= ii) if fwd else (jj >= ii)
    strict = (jj < ii) if fwd else (jj > ii)
    eye = jnp.where(jj == ii, 1.0, 0.0)
    nt = (((1,), (1,)), ((), ()))
    tn = (((0,), (0,)), ((), ()))
    nchunk = tb // CHUNK

    order = [c if fwd else nchunk - 1 - c for c in range(nchunk)]
    pairs = [(slice(c * CHUNK, (c + 1) * CHUNK), slice(g * GROUP_W, (g + 1) * GROUP_W), g)
             for c in order for g in range(ngroups)]
    dot = functools.partial(jnp.dot, preferred_element_type=F32)

    dec, kq = [], []
    for rs, ls, _ in pairs:
        dec.append(jnp.exp(jnp.where(incl, g_col[rs, ls] - g_row[rs, ls], -1e30)))
        kc16 = k[rs, ls].astype(BF16)
        kq.append(lax.dot_general(jnp.concatenate([kc16, q[rs, ls].astype(BF16)], axis=0),
                                  _block_diag(kc16), nt, preferred_element_type=F32))

    neg_l = [-jnp.where(strict, be[rs, ls] * kqp[:CHUNK] * d, 0.0)
             for (rs, ls, _), kqp, d in zip(pairs, kq, dec)]
    acc = [eye + n for n in neg_l]
    power = list(neg_l)
    n_doublings = CHUNK.bit_length() - 3
    assert 2 ** (n_doublings + 2) == CHUNK
    for it in range(n_doublings + 1):
        for p in range(len(pairs)):
            wmat = _block_diag(power[p].astype(BF16))
            if it == 0:
                power[p] = dot(power[p].astype(BF16), wmat)
            elif it < n_doublings:
                r = dot(jnp.concatenate([acc[p], power[p]], axis=0).astype(BF16), wmat)
                acc[p] = acc[p] + r[:CHUNK]
                power[p] = r[CHUNK:]
            else:
                acc[p] = acc[p] + dot(acc[p].astype(BF16), wmat)

    xs = [_split2(a) for a in acc]
    ns = [_split2(n) for n in neg_l]
    nx1 = [dot(jnp.concatenate([nh, nl], axis=0), _block_diag(xh))
           for (nh, nl), (xh, _) in zip(ns, xs)]
    nx2 = [dot(nh, _block_diag(xl)) for (nh, _), (_, xl) in zip(ns, xs)]
    resid = [eye - a + (n1[:CHUNK] + n1[CHUNK:] + n2) for a, n1, n2 in zip(acc, nx1, nx2)]
    acc = [a + dot(xh, _block_diag(rsd.astype(BF16))) for a, (xh, _), rsd in zip(acc, xs, resid)]

    eg_c = [jnp.exp(g_col[rs, ls]) for rs, ls, _ in pairs]
    wu = []
    for (rs, ls, _), a, eg in zip(pairs, acc, eg_c):
        rw = (be[rs, ls] * eg * k[rs, ls]).astype(BF16)
        ru = (be[rs, ls] * v[rs, ls]).astype(BF16)
        wu.append(dot(a.astype(BF16), jnp.concatenate([_block_diag(rw), _block_diag(ru)], axis=1)))

    lhs, qk_d, k_dec, eg_last = [], [], [], []
    for (rs, ls, _), kqp, d, eg, wup in zip(pairs, kq, dec, eg_c, wu):
        gc = g_col[rs, ls]
        g_last = gc[CHUNK - 1:CHUNK, :] if fwd else gc[0:1, :]
        lhs.append(jnp.concatenate([wup[:, :GROUP_W], q[rs, ls] * eg], axis=0).astype(BF16))
        qk_d.append((kqp[CHUNK:] * d).astype(BF16))
        k_dec.append((k[rs, ls] * jnp.exp(g_last - gc)).astype(BF16))
        eg_last.append(jnp.exp(g_last))

    for step in range(nchunk):
        ps = range(step * ngroups, (step + 1) * ngroups)
        s_bd = [s_scr[pairs[p][2]] for p in ps]
        r = [dot(lhs[p], s.astype(BF16)) for p, s in zip(ps, s_bd)]
        u16 = [(wu[p][:, GROUP_W:] - rp[:CHUNK]).astype(BF16) for p, rp in zip(ps, r)]
        o_c = [rp[CHUNK:] + dot(qk_d[p], _block_diag(u)) for p, rp, u in zip(ps, r, u16)]
        upd = [lax.dot_general(k_dec[p], u, tn, preferred_element_type=F32) for p, u in zip(ps, u16)]
        for p, s, o, up in zip(ps, s_bd, o_c, upd):
            rs, ls, g = pairs[p]
            s_scr[g] = s * eg_last[p] + same_chunk * up
            o_ref[rs, ls] = o

    if want_state:
        @pl.when(blk == nblk - 1)
        def _():
            for g in range(ngroups):
                for s in range(SEGS):
                    rows = s_scr[g, s * HEAD:(s + 1) * HEAD, :]
                    sf_ref[0, g * SEGS + s] = rows[:, s * HEAD:(s + 1) * HEAD]


def _deltanet(src, ab, conv_w, ones_bd, esel, s0_bd, batch, seq_len, fwd, want_state):
    ntok = src.shape[0]
    nblk = seq_len // TOK_TILE
    has_state = s0_bd is not None
    prepared = conv_w is None
    ngroups = DN_W // GROUP_W
    sub = TOK_TILE // 8

    def blkpos(b, i):
        return b * nblk + (i if fwd else nblk - 1 - i)

    tok_spec = lambda width: pl.BlockSpec((TOK_TILE, width), lambda b, i: (blkpos(b, i), 0))
    fixed = lambda shape: pl.BlockSpec(shape, lambda b, i: (0, 0))
    if prepared:
        in_specs = [tok_spec(3 * DN_W), tok_spec(LANES), fixed((GROUP_W, GROUP_W)),
                    fixed((3 * LANES, 2 * DN_W))]
        args = [src, ab, ones_bd, esel]
    else:
        in_specs = [
            tok_spec(3 * DN_W),
            pl.BlockSpec((8, 3 * DN_W), lambda b, i: (jnp.maximum(blkpos(b, i) * sub - 1, 0), 0)),
            pl.BlockSpec((8, 3 * DN_W),
                         lambda b, i: (jnp.minimum((blkpos(b, i) + 1) * sub, ntok // 8 - 1), 0)),
            tok_spec(LANES), fixed((3, 3 * DN_W)), fixed((GROUP_W, GROUP_W)),
            fixed((3 * LANES, 2 * DN_W)),
        ]
        args = [src, src, src, ab, conv_w, ones_bd, esel]
    if has_state:
        in_specs.append(pl.BlockSpec((1, ngroups, GROUP_W, GROUP_W), lambda b, i: (b, 0, 0, 0)))
        args.append(s0_bd)
    out_specs = [tok_spec(DN_W)]
    out_shape = [jax.ShapeDtypeStruct((ntok, DN_W), F32)]
    if not prepared:
        out_specs.append(tok_spec(3 * DN_W))
        out_shape.append(jax.ShapeDtypeStruct((ntok, 3 * DN_W), F32))
    if want_state:
        out_specs.append(pl.BlockSpec((1, N_DN_HEADS, HEAD, HEAD), lambda b, i: (b, 0, 0, 0)))
        out_shape.append(jax.ShapeDtypeStruct((batch, N_DN_HEADS, HEAD, HEAD), F32))
    return pl.pallas_call(
        functools.partial(_dn_kernel, fwd=fwd, nblk=nblk, has_state=has_state,
                          want_state=want_state, prepared=prepared),
        grid=(batch, nblk),
        in_specs=in_specs,
        out_specs=out_specs,
        out_shape=out_shape,
        scratch_shapes=[pltpu.VMEM((ngroups, GROUP_W, GROUP_W), F32)],
        compiler_params=pltpu.CompilerParams(dimension_semantics=("arbitrary", "arbitrary"),
                                             vmem_limit_bytes=VMEM_LIMIT),
        name="deltanet_fwd" if fwd else "deltanet_bwd",
    )(*args)


def _post_kernel(*refs, final):
    (x_ref, oa_ref, of_ref, ob_ref, gout_ref, gates_ref, mod_ref, wpa_ref, wpd_ref, wout_ref,
     dng_ref, ones_ref, n2_ref, w1_ref, b1_ref, w2_ref, b2_ref) = refs[:17]
    if final:
        fn_ref, o_ref = refs[17:19]
    else:
        o_ref = refs[17]
    od = of_ref[...] + ob_ref[...]
    od = od * lax.rsqrt(_seg_sumsq(od, ones_ref[...]) * (1.0 / HEAD) + EPS) * dng_ref[...]
    gout = gout_ref[...]
    od = od * (gout * _sigmoid(gout))
    pa = jnp.dot(oa_ref[...], wpa_ref[...], preferred_element_type=F32)
    pd = jnp.dot(od.astype(BF16), wpd_ref[...], preferred_element_type=F32)
    merged = gates_ref[:, 0:D_MODEL] * pa + gates_ref[:, D_MODEL:2 * D_MODEL] * pd
    gate1 = mod_ref[0, :, 2 * D_MODEL:3 * D_MODEL]
    x = x_ref[...] + gate1 * jnp.dot(merged.astype(BF16), wout_ref[...],
                                     preferred_element_type=F32)

    shift2 = mod_ref[0, :, 3 * D_MODEL:4 * D_MODEL]
    scale2 = mod_ref[0, :, 4 * D_MODEL:5 * D_MODEL]
    gate2 = mod_ref[0, :, 5 * D_MODEL:6 * D_MODEL]
    h = _rms_rows(x) * n2_ref[...]
    h = h * (1.0 + scale2) + shift2
    a = jnp.dot(h.astype(BF16), w1_ref[...], preferred_element_type=F32) + b1_ref[...]
    a = jnp.square(jnp.maximum(a, 0.0))
    ff = jnp.dot(a.astype(BF16), w2_ref[...], preferred_element_type=F32) + b2_ref[...]
    out = x + gate2 * ff
    if final:
        out = _rms_rows(out) * fn_ref[...]
    o_ref[...] = out


def _post(x2d, oa, o_f, o_b, dcat, gates, mod, tiles_per_mod, wpa, wpd, wout, dng, ones_bd, n2, w1,
          b1, w2, b2, final_norm):
    ntok = x2d.shape[0]
    final = final_norm is not None
    row = lambda i: (i, 0)
    fixed = lambda shape: pl.BlockSpec(shape, lambda i: (0, 0), pipeline_mode=pl.Buffered(1))
    in_specs = [
        pl.BlockSpec((TOK_TILE, D_MODEL), row),
        pl.BlockSpec((TOK_TILE, ATTN_W), row),
        pl.BlockSpec((TOK_TILE, DN_W), row),
        pl.BlockSpec((TOK_TILE, DN_W), row),
        pl.BlockSpec((TOK_TILE, DN_W), lambda i: (i, 3)),
        pl.BlockSpec((TOK_TILE, 2 * D_MODEL), row),
        pl.BlockSpec((1, 1, 6 * D_MODEL), lambda i: (i // tiles_per_mod, 0, 0)),
        fixed((ATTN_W, D_MODEL)),
        fixed((DN_W, D_MODEL)),
        fixed((D_MODEL, D_MODEL)),
        fixed((1, DN_W)),
        fixed((GROUP_W, GROUP_W)),
        fixed((1, D_MODEL)),
        fixed((D_MODEL, D_FF)),
        fixed((1, D_FF)),
        fixed((D_FF, D_MODEL)),
        fixed((1, D_MODEL)),
    ]
    args = [x2d, oa, o_f, o_b, dcat, gates, mod, wpa, wpd, wout, dng, ones_bd, n2, w1, b1, w2, b2]
    if final:
        in_specs.append(fixed((1, D_MODEL)))
        args.append(final_norm)
    return pl.pallas_call(
        functools.partial(_post_kernel, final=final),
        grid=(ntok // TOK_TILE,),
        in_specs=in_specs,
        out_specs=pl.BlockSpec((TOK_TILE, D_MODEL), row),
        out_shape=jax.ShapeDtypeStruct((ntok, D_MODEL), F32),
        compiler_params=pltpu.CompilerParams(dimension_semantics=("arbitrary",),
                                             vmem_limit_bytes=VMEM_LIMIT),
        name="merge_mlp",
    )(*args)


def _rope_tables(n_tokens):
    freqs = HEAD // 4
    t = jnp.arange(n_tokens)
    row = (t // GRID_W).astype(F32)
    col = (t % GRID_W).astype(F32)
    inv = ROPE_THETA ** (-jnp.arange(freqs, dtype=F32) / freqs)
    ang = jnp.concatenate([row[:, None] * inv, col[:, None] * inv], axis=-1)
    cos, sin = jnp.cos(ang), jnp.sin(ang)
    cos_h = jnp.concatenate([cos, cos], axis=-1)
    sin_h = jnp.concatenate([-sin, sin], axis=-1)
    reps = LANES // HEAD
    return jnp.tile(cos_h, (1, reps)), jnp.tile(sin_h, (1, reps))


def _pack_w_in(w):
    pad = jnp.zeros((D_MODEL, LANES - 4 * N_DN_HEADS), w.dtype)
    return jnp.concatenate([w[:, :C_AB], w[:, C_AB:C_AB + 4 * N_DN_HEADS], pad,
                            w[:, C_AB + 4 * N_DN_HEADS:]], axis=1).astype(BF16)


def _pad_lanes(v):
    v = v.reshape(1, -1)
    return jnp.pad(v, ((0, 0), (0, LANES - v.shape[1])))


def _expand_selector(direction):
    src = jnp.arange(LANES)[:, None]
    dst = jnp.arange(2 * DN_W)[None, :]
    head = (dst % DN_W) // HEAD
    want = jnp.where(dst < DN_W, direction * N_DN_HEADS + head,
                     2 * N_DN_HEADS + direction * N_DN_HEADS + head)
    return jnp.tile((src == want).astype(BF16), (3, 1))


def _kv_head_orders(k, v):
    k2 = jnp.concatenate([k[:, 0], k[:, 1], k[:, 1], k[:, 0]], axis=-1)
    ones = jnp.ones(v.shape[:1] + v.shape[2:3] + (V_ROWS - HEAD,), v.dtype)
    vt = jnp.concatenate([v[:, 0], ones, v[:, 1], ones], axis=-1)
    return k2.astype(BF16), vt.transpose(0, 2, 1).astype(BF16)


def _state_to_block_diag(s):
    b = s.shape[0]
    s = s.reshape(b, DN_W // GROUP_W, SEGS, HEAD, HEAD).astype(F32)
    eye = jnp.eye(SEGS, dtype=F32)
    bd = s[:, :, :, :, None, :] * eye[None, None, :, None, :, None]
    return bd.reshape(b, DN_W // GROUP_W, GROUP_W, GROUP_W)


def kernel(x_prompt, x_sample, c, cache_k, cache_v, state_delta, c_ctx, w_mod, b_mod, norm1, norm2,
           w_in, conv_w, q_gain, k_gain, a_log, dt_bias, dn_gain, w_pa, w_pd, w_out, w1, b1, w2, b2,
           final_norm):
    depth = w_in.shape[0]
    batch, seq, _ = x_prompt.shape
    dec_batch, dec_seq, _ = x_sample.shape
    assert seq == TOK_TILE and dec_seq % TOK_TILE == 0

    mod_rows = 8
    cvec = jnp.concatenate([c, c_ctx[None, :],
                            jnp.zeros((mod_rows - dec_batch - 1, D_MODEL), F32)], axis=0)
    mod_all = _modulation(cvec, w_mod.astype(BF16), b_mod)

    ones_bd = (jnp.arange(GROUP_W)[:, None] // HEAD == jnp.arange(GROUP_W)[None, :] // HEAD
               ).astype(BF16)
    esel = [_expand_selector(0), _expand_selector(1)]
    rope_tabs = _rope_tables(dec_seq)

    xp = x_prompt.reshape(batch * seq, D_MODEL)
    xs = x_sample.reshape(dec_batch * dec_seq, D_MODEL)
    ks_out, vs_out, st_out = [], [], []
    for l in range(depth):
        w_cat = _pack_w_in(w_in[l])
        wvt = w_in[l][:, ATTN_W + KV_W:ATTN_W + 2 * KV_W].T.astype(BF16)
        n1 =norm1[l].reshape(1, D_MODEL)
        n2 = norm2[l].reshape(1, D_MODEL)
        qg = jnp.tile(q_gain[l], N_Q_HEADS).reshape(1, ATTN_W)
        kg = jnp.tile(k_gain[l], N_KV_HEADS).reshape(1, KV_W)
        dng = jnp.tile(dn_gain[l], N_DN_HEADS).reshape(1, DN_W)
        alog = _pad_lanes(a_log[l])
        dtb = _pad_lanes(dt_bias[l])
        wpa, wpd, wout = w_pa[l].astype(BF16), w_pd[l].astype(BF16), w_out[l].astype(BF16)
        w1b, w2b = w1[l].astype(BF16), w2[l].astype(BF16)
        b1r, b2r = b1[l].reshape(1, D_FF), b2[l].reshape(1, D_MODEL)
        last = l == depth - 1
        fin = final_norm.reshape(1, D_MODEL) if last else None
        mod_lat = mod_all[l, :dec_batch].reshape(dec_batch, 1, 6 * D_MODEL)
        mod_ctx = mod_all[l, dec_batch:dec_batch + 1].reshape(1, 1, 6 * D_MODEL)

        tpm = batch * seq // TOK_TILE
        q, k2, vt, dcat, ab, gates, k_c, v_c = _inproj(xp, mod_ctx, tpm, n1, w_cat, wvt, qg, kg,
                                                       alog, dtb, ones_bd, None, True, seq)
        oa = _attention(q, k2, vt, None, None, batch, seq)
        o_f, qkv, s_f = _deltanet(dcat, ab, conv_w[l], ones_bd, esel[0], None, batch, seq, True, True)
        o_b, s_b = _deltanet(qkv, ab, None, ones_bd, esel[1], None, batch, seq, False, True)
        xp = _post(xp, oa, o_f, o_b, dcat, gates, mod_ctx, tpm, wpa, wpd, wout, dng, ones_bd, n2,
                   w1b, b1r, w2b, b2r, fin)
        ks_out.append(k_c)
        vs_out.append(v_c)
        st_out.append(jnp.stack([s_f, s_b], axis=1))

        tpm = dec_seq // TOK_TILE
        k2c, vtc = _kv_head_orders(cache_k[:, l], cache_v[:, l])
        q, k2, vt, dcat, ab, gates = _inproj(xs, mod_lat, tpm, n1, w_cat, wvt, qg, kg, alog, dtb,
                                             ones_bd, rope_tabs, False, dec_seq)
        oa = _attention(q, k2, vt, k2c, vtc, dec_batch, dec_seq)
        s0f = _state_to_block_diag(state_delta[:, l, 0])
        s0b = _state_to_block_diag(state_delta[:, l, 1])
        o_f, qkv = _deltanet(dcat, ab, conv_w[l], ones_bd, esel[0], s0f, dec_batch, dec_seq, True,
                             False)
        (o_b,) = _deltanet(qkv, ab, None, ones_bd, esel[1], s0b, dec_batch, dec_seq, False, False)
        xs = _post(xs, oa, o_f, o_b, dcat, gates, mod_lat, tpm, wpa, wpd, wout, dng, ones_bd, n2,
                   w1b, b1r, w2b, b2r, fin)

    y_prompt = xp.reshape(batch, seq, D_MODEL)
    y_sample = xs.reshape(dec_batch, dec_seq, D_MODEL)
    return (y_prompt, y_sample, jnp.stack(ks_out, axis=1), jnp.stack(vs_out, axis=1),
            jnp.stack(st_out, axis=1))
```

```python
import functools

import jax
import jax.numpy as jnp
from jax import lax
from jax.experimental import pallas as pl
from jax.experimental.pallas import tpu as pltpu

F32 = jnp.float32
BF16 = jnp.bfloat16

D_MODEL = 1024
HEAD = 64
N_Q_HEADS = 8
N_KV_HEADS = 2
N_DN_HEADS = 8
ATTN_W = N_Q_HEADS * HEAD
KV_W = N_KV_HEADS * HEAD
DN_W = N_DN_HEADS * HEAD
CHUNK = 64
GRID_W = 64
D_FF = 4 * D_MODEL
ROPE_THETA = 10000.0
EPS = 1e-6
LOG2_E = 1.4426950408889634

LANES = 128
GROUP_W = 256
SEGS = GROUP_W // HEAD
TOK_TILE = 256
V_ROWS = HEAD + 16
VT_ROWS = N_KV_HEADS * V_ROWS
ATTN_CHUNK_TILES = 2
ATTN_Q_TILE = 512
DN_TILE = 512
VMEM_LIMIT = 56 * 1024 * 1024

C_A = 0
C_D = ATTN_W + 2 * KV_W
C_AB = C_D + 4 * DN_W
C_G = C_AB + LANES
C_END = C_G + 2 * D_MODEL


def _sigmoid(x):
    return 1.0 / (1.0 + jnp.exp(-x))


def _bdot(a, b):
    return jnp.dot(a.astype(BF16), b.astype(BF16), preferred_element_type=F32)


def _split2(x):
    hi = x.astype(BF16)
    return hi, (x - hi.astype(F32)).astype(BF16)


def _split3(x):
    hi = x.astype(BF16)
    r = x - hi.astype(F32)
    mid = r.astype(BF16)
    return hi, mid, (r - mid.astype(F32)).astype(BF16)


def _sel_dot_left(x, sel3):
    return jnp.dot(jnp.concatenate(_split3(x), axis=1), sel3, preferred_element_type=F32)


def _sel_dot_right(sel3, x):
    return jnp.dot(sel3, jnp.concatenate(_split3(x), axis=0), preferred_element_type=F32)


def _seg_sumsq(x, ones_bd):
    x2 = x * x
    hi = x2.astype(BF16)
    lo = (x2 - hi.astype(F32)).astype(BF16)
    width = x.shape[1]
    if width == LANES:
        m = ones_bd[:LANES, :LANES]
        return (jnp.dot(hi, m, preferred_element_type=F32)
                + jnp.dot(lo, m, preferred_element_type=F32))
    outs = []
    for g in range(width // GROUP_W):
        sl = slice(g * GROUP_W, (g + 1) * GROUP_W)
        outs.append(jnp.dot(hi[:, sl], ones_bd, preferred_element_type=F32)
                    + jnp.dot(lo[:, sl], ones_bd, preferred_element_type=F32))
    return jnp.concatenate(outs, axis=1)


def _rms_rows(x):
    return x * lax.rsqrt(jnp.mean(x * x, axis=-1, keepdims=True) + EPS)


def _mod_kernel(c_ref, w_ref, b_ref, o_ref):
    c = c_ref[...]
    s = c * _sigmoid(c)
    o_ref[0] = jnp.dot(s.astype(BF16), w_ref[0], preferred_element_type=F32) + b_ref[0]


def _modulation(cvec, w_mod_bf, b_mod):
    depth = w_mod_bf.shape[0]
    rows = cvec.shape[0]
    nblk = 6 * D_MODEL // D_MODEL
    return pl.pallas_call(
        _mod_kernel,
        grid=(depth, nblk),
        in_specs=[
            pl.BlockSpec((rows, D_MODEL), lambda l, j: (0, 0)),
            pl.BlockSpec((1, D_MODEL, D_MODEL), lambda l, j: (l, 0, j)),
            pl.BlockSpec((1, 1, D_MODEL), lambda l, j: (l, 0, j)),
        ],
        out_specs=pl.BlockSpec((1, rows, D_MODEL), lambda l, j: (l, 0, j)),
        out_shape=jax.ShapeDtypeStruct((depth, rows, 6 * D_MODEL), F32),
        name="adaln_mod",
    )(cvec, w_mod_bf, b_mod.reshape(depth, 1, 6 * D_MODEL))


def _rope(x, cos, sin_signed):
    width = x.shape[1]
    lane = lax.broadcasted_iota(jnp.int32, x.shape, 1)
    first_half = (lane % HEAD) < (HEAD // 2)
    rot = jnp.where(first_half, pltpu.roll(x, width - HEAD // 2, 1), pltpu.roll(x, HEAD // 2, 1))
    return x * cos + rot * sin_signed


def _inproj_kernel(*refs, rope, cache_out):
    (x_ref, mod_ref, n1_ref, w_ref, wvt_ref, qg_ref, kg_ref, alog_ref, dtb_ref,
     ones_ref) = refs[:10]
    pos = 10
    if rope:
        cos_ref, sin_ref = refs[pos:pos + 2]
        pos += 2
    q_ref, k2_ref, vt_ref, d_ref, ab_ref, g_ref = refs[pos:pos + 6]
    pos += 6
    if cache_out:
        kc_ref, vc_ref = refs[pos:pos + 2]

    shift1 = mod_ref[0, :, 0:D_MODEL]
    scale1 = mod_ref[0, :, D_MODEL:2 * D_MODEL]
    h = _rms_rows(x_ref[...]) * n1_ref[...]
    h = h * (1.0 + scale1) + shift1
    h16 = h.astype(BF16)
    y = jnp.dot(h16, w_ref[...], preferred_element_type=F32)
    ones_bd = ones_ref[...]

    vt = lax.dot_general(wvt_ref[...], h16, (((1,), (1,)), ((), ())),
                         preferred_element_type=F32).astype(BF16)
    ones = jnp.ones((V_ROWS - HEAD, vt.shape[1]), BF16)
    for kv in range(N_KV_HEADS):
        vt_ref[0, kv * V_ROWS:kv * V_ROWS + HEAD, :] = vt[kv * HEAD:(kv + 1) * HEAD]
        vt_ref[0, kv * V_ROWS + HEAD:(kv + 1) * V_ROWS, :] = ones

    qa = y[:, C_A:C_A + ATTN_W]
    qn = qa * lax.rsqrt(_seg_sumsq(qa, ones_bd) * (1.0 / HEAD) + EPS) * qg_ref[...]
    ka = y[:, C_A + ATTN_W:C_A + ATTN_W + KV_W]
    kn = ka * lax.rsqrt(_seg_sumsq(ka, ones_bd) * (1.0 / HEAD) + EPS) * kg_ref[...]
    va = y[:, C_A + ATTN_W + KV_W:C_D]
    if cache_out:
        kc_ref[0, 0] = kn[:, :HEAD]
        kc_ref[0, 1] = kn[:, HEAD:]
        vc_ref[0, 0] = va[:, :HEAD]
        vc_ref[0, 1] = va[:, HEAD:]
    if rope:
        cos = cos_ref[...]
        sin = sin_ref[...]
        qn = _rope(qn, jnp.concatenate([cos] * (ATTN_W // LANES), axis=1),
                   jnp.concatenate([sin] * (ATTN_W // LANES), axis=1))
        kn = _rope(kn, cos, sin)
    q_ref[...] = (qn * (HEAD ** -0.5 * LOG2_E)).astype(BF16)
    k2_ref[0, :, 0:LANES] = kn.astype(BF16)
    k2_ref[0, :, LANES:2 * LANES] = pltpu.roll(kn, HEAD, 1).astype(BF16)

    d_ref[...] = y[:, C_D:C_AB]

    z = y[:, C_AB:C_G]
    zb = z + dtb_ref[...]
    softplus = jnp.maximum(zb, 0.0) + jnp.log(1.0 + jnp.exp(-jnp.abs(zb)))
    log_a = -jnp.exp(alog_ref[...]) * softplus
    lane = lax.broadcasted_iota(jnp.int32, z.shape, 1)
    ab_ref[...] = jnp.where(lane < 2 * N_DN_HEADS, log_a,
                            jnp.where(lane < 4 * N_DN_HEADS, _sigmoid(z), 0.0))

    g_ref[...] = _sigmoid(y[:, C_G:C_END]).astype(BF16)


def _inproj(x2d, mod, tiles_per_mod, n1, w_cat, wvt, qg, kg, alog, dtb, ones_bd, rope_tabs,
            cache_out, seq_len):
    ntok = x2d.shape[0]
    ntiles = ntok // TOK_TILE
    rope = rope_tabs is not None
    const = lambda i: (0, 0)
    row = lambda i: (i, 0)
    in_specs = [
        pl.BlockSpec((TOK_TILE, D_MODEL), row),
        pl.BlockSpec((1, 1, 6 * D_MODEL), lambda i: (i // tiles_per_mod, 0, 0)),
        pl.BlockSpec((1, D_MODEL), const),
        pl.BlockSpec((D_MODEL, C_END), const),
        pl.BlockSpec((KV_W, D_MODEL), const),
        pl.BlockSpec((1, ATTN_W), const),
        pl.BlockSpec((1, KV_W), const),
        pl.BlockSpec((1, LANES), const),
        pl.BlockSpec((1, LANES), const),
        pl.BlockSpec((GROUP_W, GROUP_W), const),
    ]
    args = [x2d, mod, n1, w_cat, wvt, qg, kg, alog, dtb, ones_bd]
    if rope:
        tiles_per_seq = seq_len // TOK_TILE
        in_specs += [pl.BlockSpec((TOK_TILE, LANES), lambda i: (i % tiles_per_seq, 0))] * 2
        args += list(rope_tabs)
    out_specs = [
        pl.BlockSpec((TOK_TILE, ATTN_W), row),
        pl.BlockSpec((1, TOK_TILE, 2 * LANES), lambda i: (i, 0, 0)),
        pl.BlockSpec((1, VT_ROWS, TOK_TILE), lambda i: (i, 0, 0)),
        pl.BlockSpec((TOK_TILE, 4 * DN_W), row),
        pl.BlockSpec((TOK_TILE, LANES), row),
        pl.BlockSpec((TOK_TILE, 2 * D_MODEL), row),
    ]
    out_shape = [
        jax.ShapeDtypeStruct((ntok, ATTN_W), BF16),
        jax.ShapeDtypeStruct((ntiles, TOK_TILE, 2 * LANES), BF16),
        jax.ShapeDtypeStruct((ntiles, VT_ROWS, TOK_TILE), BF16),
        jax.ShapeDtypeStruct((ntok, 4 * DN_W), F32),
        jax.ShapeDtypeStruct((ntok, LANES), F32),
        jax.ShapeDtypeStruct((ntok, 2 * D_MODEL), BF16),
    ]
    if cache_out:
        assert seq_len == TOK_TILE
        out_specs += [pl.BlockSpec((1, N_KV_HEADS, TOK_TILE, HEAD), lambda i: (i, 0, 0, 0))] * 2
        out_shape += [jax.ShapeDtypeStruct((ntiles, N_KV_HEADS, TOK_TILE, HEAD), F32)] * 2
    return pl.pallas_call(
        functools.partial(_inproj_kernel, rope=rope, cache_out=cache_out),
        grid=(ntiles,),
        in_specs=in_specs,
        out_specs=out_specs,
        out_shape=out_shape,
        compiler_params=pltpu.CompilerParams(dimension_semantics=("arbitrary",),
                                             vmem_limit_bytes=VMEM_LIMIT),
        name="inproj",
    )(*args)


def _attn_kernel(*refs, has_ctx):
    if has_ctx:
        q_ref, k2_ref, vt_ref, k2c_ref, vtc_ref, o_ref, s_scr, p_scr, acc_scr = refs
    else:
        q_ref, k2_ref, vt_ref, o_ref, s_scr, p_scr, acc_scr = refs
    tq = q_ref.shape[0]
    lane = lax.broadcasted_iota(jnp.int32, (1, LANES), 1)
    lower = lane < HEAD
    nt = (((1,), (1,)), ((), ()))
    qh, sel, vrow = [], [], []
    for head in range(N_Q_HEADS):
        j, half = divmod(head, 2)
        kv = head // (N_Q_HEADS // N_KV_HEADS)
        qb = q_ref[:, j * LANES:(j + 1) * LANES]
        qh.append(jnp.where(lower if half == 0 else jnp.logical_not(lower), qb, jnp.zeros_like(qb)))
        sel.append(0 if half == kv else 1)
        vrow.append(kv * V_ROWS)

    heads = range(N_Q_HEADS)
    tiles = s_scr.shape[2] // TOK_TILE
    nchunks = k2_ref.shape[0] // tiles

    def is_ctx(c):
        return has_ctx and isinstance(c, int) and c < 0

    def key_tiles(c):
        if is_ctx(c):
            return [(k2c_ref[0], vtc_ref[0])]
        return [(k2_ref[c * tiles + t], vt_ref[c * tiles + t]) for t in range(tiles)]

    def put_scores(c, slot):
        k_chunk = jnp.concatenate([kt for kt, _ in key_tiles(c)], axis=0)
        for h in heads:
            s_scr[slot, h, 0:k_chunk.shape[0]] = lax.dot_general(
                k_chunk[:, sel[h] * LANES:(sel[h] + 1) * LANES], qh[h], nt,
                preferred_element_type=F32)

    def add_values(c, slot, alpha):
        vts = [vt for _, vt in key_tiles(c)]
        for h in heads:
            pv = alpha[h] * acc_scr[h]
            for t, vt_tile in enumerate(vts):
                pv = pv + jnp.dot(vt_tile[vrow[h]:vrow[h] + V_ROWS, :],
                                  p_scr[slot, h, t * TOK_TILE:(t + 1) * TOK_TILE],
                                  preferred_element_type=F32)
            acc_scr[h] = pv

    def softmax(c, slot, stats):
        rows = k2c_ref.shape[1] if is_ctx(c) else tiles * TOK_TILE
        new_stats, alpha = [], []
        for h in heads:
            sx = s_scr[slot, h, 0:rows]
            m_new = jnp.maximum(stats[h], jnp.max(sx, axis=0, keepdims=True))
            p_scr[slot, h, 0:rows] = jnp.exp2(sx - m_new).astype(BF16)
            alpha.append(jnp.exp2(stats[h] - m_new))
            new_stats.append(m_new)
        return tuple(new_stats), tuple(alpha)

    def step(c, parity, stats, alpha, has_prev, has_next):
        if has_next:
            put_scores(c + 1, 1 - parity)
        if has_prev:
            add_values(c - 1, 1 - parity, alpha)
        return softmax(c, parity, stats)

    first = -1 if has_ctx else 0
    last = nchunks - 1
    stats = tuple(jnp.full((1, tq), -1e30, F32) for _ in heads)
    alpha = tuple(jnp.zeros((1, tq), F32) for _ in heads)
    acc_scr[...] = jnp.zeros_like(acc_scr)
    put_scores(first, first % 2)
    c = first
    while c <= min(0, last):
        stats, alpha = step(c, c % 2, stats, alpha, c > first, c < last)
        c += 1
    per_trip = 2
    ntrips = max(last - c, 0) // per_trip

    def body(i, carry):
        c0 = c + per_trip * i
        for j in range(per_trip):
            carry = step(c0 + j, (c + j) % 2, *carry, True, True)
        return carry

    if ntrips > 0:
        stats, alpha = lax.fori_loop(0, ntrips, body, (stats, alpha))
        c += per_trip * ntrips
    while c <= last:
        stats, alpha = step(c, c % 2, stats, alpha, c > first, c < last)
        c += 1
    add_values(last, last % 2, alpha)
    res = []
    for h in heads:
        acc = acc_scr[h]
        res.append(acc[0:HEAD, :] / acc[HEAD:HEAD + 1, :])
    blocks = [jnp.concatenate(res[2 * j:2 * j + 2], axis=0).T for j in range(ATTN_W // LANES)]
    o_ref[...] = jnp.concatenate(blocks, axis=1).astype(BF16)


def _attention(q, k2, vt, k2c, vtc, batch, seq_len):
    ntok = q.shape[0]
    nkt = seq_len // TOK_TILE
    has_ctx = k2c is not None
    chunk_keys = ATTN_CHUNK_TILES * TOK_TILE if nkt % ATTN_CHUNK_TILES == 0 else TOK_TILE
    tq = min(ATTN_Q_TILE, seq_len)
    nq = seq_len // tq
    in_specs = [
        pl.BlockSpec((tq, ATTN_W), lambda b, i: (b * nq + i, 0)),
        pl.BlockSpec((nkt, TOK_TILE, 2 * LANES), lambda b, i: (b, 0, 0)),
        pl.BlockSpec((nkt, VT_ROWS, TOK_TILE), lambda b, i: (b, 0, 0)),
    ]
    args = [q, k2, vt]
    if has_ctx:
        in_specs += [pl.BlockSpec((1,) + k2c.shape[1:], lambda b, i: (b, 0, 0)),
                     pl.BlockSpec((1,) + vtc.shape[1:], lambda b, i: (b, 0, 0))]
        args += [k2c, vtc]
    return pl.pallas_call(
        functools.partial(_attn_kernel, has_ctx=has_ctx),
        grid=(batch, nq),
        in_specs=in_specs,
        out_specs=pl.BlockSpec((tq, ATTN_W), lambda b, i: (b * nq + i, 0)),
        out_shape=jax.ShapeDtypeStruct((ntok, ATTN_W), BF16),
        scratch_shapes=[pltpu.VMEM((2, N_Q_HEADS, chunk_keys, tq), F32),
                        pltpu.VMEM((2, N_Q_HEADS, chunk_keys, tq), BF16),
                        pltpu.VMEM((N_Q_HEADS, V_ROWS, tq), F32)],
        compiler_params=pltpu.CompilerParams(dimension_semantics=("arbitrary", "arbitrary"),
                                             vmem_limit_bytes=VMEM_LIMIT),
        name="attention",
    )(*args)


def _block_diag(x):
    seg = lax.broadcasted_iota(jnp.int32, x.shape, 1) // HEAD
    return jnp.concatenate([jnp.where(seg == s, x, jnp.zeros_like(x)) for s in range(SEGS)], axis=0)


def _dn_kernel(*refs, fwd, nblk, has_state, want_state, prepared):
    if prepared:
        qkv_ref, ab_ref, ones_ref, esel_ref = refs[:4]
        pos = 4
    else:
        x_ref, xp_ref, xn_ref, ab_ref, cw_ref, ones_ref, esel_ref = refs[:7]
        pos = 7
    if has_state:
        s0_ref = refs[pos]
        pos += 1
    o_ref = refs[pos]
    pos += 1
    if not prepared:
        qkv_out_ref = refs[pos]
        pos += 1
    if want_state:
        sf_ref = refs[pos]
        pos += 1
    s_scr = refs[pos]

    tb = o_ref.shape[0]
    ngroups = DN_W // GROUP_W
    blk = pl.program_id(1)
    rb = blk if fwd else nblk - 1 - blk

    @pl.when(blk == 0)
    def _():
        if has_state:
            s_scr[...] = s0_ref[0]
        else:
            s_scr[...] = jnp.zeros_like(s_scr)

    ones_bd = ones_ref[...]
    if prepared:
        q = qkv_ref[:, 0:DN_W]
        k = qkv_ref[:, DN_W:2 * DN_W]
        v = qkv_ref[:, 2 * DN_W:3 * DN_W]
    else:
        x = x_ref[...]
        rows = lax.broadcasted_iota(jnp.int32, (tb, 1), 0)
        prev = jnp.where(rb > 0, xp_ref[7:8, :], 0.0)
        nxt = jnp.where(rb < nblk - 1, xn_ref[0:1, :], 0.0)
        xm1 = jnp.where(rows == 0, prev, pltpu.roll(x, 1, 0))
        xp1 = jnp.where(rows == tb - 1, nxt, pltpu.roll(x, tb - 1, 0))
        y = xm1 * cw_ref[0:1, :] + x * cw_ref[1:2, :] + xp1 * cw_ref[2:3, :]
        y = y * _sigmoid(y)
        q = y[:, 0:DN_W]
        k = y[:, DN_W:2 * DN_W]
        v = y[:, 2 * DN_W:3 * DN_W]
        q = q * lax.rsqrt(_seg_sumsq(q, ones_bd) + EPS) * (HEAD ** -0.5)
        k = k * lax.rsqrt(_seg_sumsq(k, ones_bd) + EPS)
        qkv_out_ref[:, 0:DN_W] = q
        qkv_out_ref[:, DN_W:2 * DN_W] = k
        qkv_out_ref[:, 2 * DN_W:3 * DN_W] = v

    ex = _sel_dot_left(ab_ref[...], esel_ref[...])
    la = ex[:, 0:DN_W]
    be = ex[:, DN_W:2 * DN_W]
    same_chunk = ones_bd.astype(F32)
    r_i = lax.broadcasted_iota(jnp.int32, (GROUP_W, GROUP_W), 0) % CHUNK
    c_i = lax.broadcasted_iota(jnp.int32, (GROUP_W, GROUP_W), 1) % CHUNK
    tri = jnp.where((c_i <= r_i) if fwd else (c_i >= r_i), ones_bd, jnp.zeros_like(ones_bd))
    tri3 = jnp.concatenate([tri] * 3, axis=1)
    ones3 = jnp.concatenate([ones_bd] * 3, axis=1)
    t_i = lax.broadcasted_iota(jnp.int32, (GROUP_W, DN_W), 0) % CHUNK
    l_i = lax.broadcasted_iota(jnp.int32, (GROUP_W, DN_W), 1) % CHUNK
    row_mask = (t_i <= l_i) if fwd else (t_i >= l_i)
    la_parts = [la[r:r + GROUP_W] for r in range(0, tb, GROUP_W)]
    g_col = jnp.concatenate([_sel_dot_right(tri3, part) for part in la_parts], axis=0)
    g_row = jnp.concatenate([_sel_dot_right(ones3, jnp.where(row_mask, part, 0.0))
                             for part in la_parts], axis=0)

    ii = lax.broadcasted_iota(jnp.int32, (CHUNK, GROUP_W), 0)
    jj = lax.broadcasted_iota(jnp.int32, (CHUNK, GROUP_W), 1) % HEAD
    incl = (jj <= ii) if fwd else (jj >= ii)
    strict = (jj < ii) if fwd else (jj > ii)
    eye = jnp.where(jj == ii, 1.0, 0.0)
    nt = (((1,), (1,)), ((), ()))
    tn = (((0,), (0,)), ((), ()))
    nchunk = tb // CHUNK

    order = [c if fwd else nchunk - 1 - c for c in range(nchunk)]
    pairs = [(slice(c * CHUNK, (c + 1) * CHUNK), slice(g * GROUP_W, (g + 1) * GROUP_W), g)
             for c in order for g in range(ngroups)]
    dot = functools.partial(jnp.dot, preferred_element_type=F32)

    dec, kq = [], []
    for rs, ls, _ in pairs:
        dec.append(jnp.exp(jnp.where(incl, g_col[rs, ls] - g_row[rs, ls], -1e30)))
        kc16 = k[rs, ls].astype(BF16)
        kq.append(lax.dot_general(jnp.concatenate([kc16, q[rs, ls].astype(BF16)], axis=0),
                                  _block_diag(kc16), nt, preferred_element_type=F32))

    neg_l = [-jnp.where(strict, be[rs, ls] * kqp[:CHUNK] * d, 0.0)
             for (rs, ls, _), kqp, d in zip(pairs, kq, dec)]
    acc = [eye + n for n in neg_l]
    power = list(neg_l)
    n_doublings = CHUNK.bit_length() - 3
    assert 2 ** (n_doublings + 2) == CHUNK
    for it in range(n_doublings + 1):
        for p in range(len(pairs)):
            wmat = _block_diag(power[p].astype(BF16))
            if it == 0:
                power[p] = dot(power[p].astype(BF16), wmat)
            elif it < n_doublings:
                r = dot(jnp.concatenate([acc[p], power[p]], axis=0).astype(BF16), wmat)
                acc[p] = acc[p] + r[:CHUNK]
                power[p] = r[CHUNK:]
            else:
                acc[p] = acc[p] + dot(acc[p].astype(BF16), wmat)

    xs = [_split2(a) for a in acc]
    ns = [_split2(n) for n in neg_l]
    nx1 = [dot(jnp.concatenate([nh, nl], axis=0), _block_diag(xh))
           for (nh, nl), (xh, _) in zip(ns, xs)]
    nx2 = [dot(nh, _block_diag(xl)) for (nh, _), (_, xl) in zip(ns, xs)]
    resid = [eye - a + (n1[:CHUNK] + n1[CHUNK:] + n2) for a, n1, n2 in zip(acc, nx1, nx2)]
    acc = [a + dot(xh, _block_diag(rsd.astype(BF16))) for a, (xh, _), rsd in zip(acc, xs, resid)]

    eg_c = [jnp.exp(g_col[rs, ls]) for rs, ls, _ in pairs]
    wu = []
    for (rs, ls, _), a, eg in zip(pairs, acc, eg_c):
        rw = (be[rs, ls] * eg * k[rs, ls]).astype(BF16)
        ru = (be[rs, ls] * v[rs, ls]).astype(BF16)
        wu.append(dot(a.astype(BF16), jnp.concatenate([_block_diag(rw), _block_diag(ru)], axis=1)))

    lhs, qk_d, k_dec, eg_last = [], [], [], []
    for (rs, ls, _), kqp, d, eg, wup in zip(pairs, kq, dec, eg_c, wu):
        gc = g_col[rs, ls]
        g_last = gc[CHUNK - 1:CHUNK, :] if fwd else gc[0:1, :]
        lhs.append(jnp.concatenate([wup[:, :GROUP_W], q[rs, ls] * eg], axis=0).astype(BF16))
        qk_d.append((kqp[CHUNK:] * d).astype(BF16))
        k_dec.append((k[rs, ls] * jnp.exp(g_last - gc)).astype(BF16))
        eg_last.append(jnp.exp(g_last))

    for step in range(nchunk):
        ps = range(step * ngroups, (step + 1) * ngroups)
        s_bd = [s_scr[pairs[p][2]] for p in ps]
        r = [dot(lhs[p], s.astype(BF16)) for p, s in zip(ps, s_bd)]
        u16 = [(wu[p][:, GROUP_W:] - rp[:CHUNK]).astype(BF16) for p, rp in zip(ps, r)]
        o_c = [rp[CHUNK:] + dot(qk_d[p], _block_diag(u)) for p, rp, u in zip(ps, r, u16)]
        upd = [lax.dot_general(k_dec[p], u, tn, preferred_element_type=F32) for p, u in zip(ps, u16)]
        for p, s, o, up in zip(ps, s_bd, o_c, upd):
            rs, ls, g = pairs[p]
            s_scr[g] = s * eg_last[p] + same_chunk * up
            o_ref[rs, ls] = o

    if want_state:
        @pl.when(blk == nblk - 1)
        def _():
            for g in range(ngroups):
                for s in range(SEGS):
                    rows = s_scr[g, s * HEAD:(s + 1) * HEAD, :]
                    sf_ref[0, g * SEGS + s] = rows[:, s * HEAD:(s + 1) * HEAD]


def _deltanet(src, ab, conv_w, ones_bd, esel, s0_bd, batch, seq_len, fwd, want_state):
    ntok = src.shape[0]
    tb = min(DN_TILE, seq_len)
    nblk = seq_len // tb
    has_state = s0_bd is not None
    prepared = conv_w is None
    ngroups = DN_W // GROUP_W
    sub = tb // 8

    def blkpos(b, i):
        return b * nblk + (i if fwd else nblk - 1 - i)

    tok_spec = lambda width: pl.BlockSpec((tb, width), lambda b, i: (blkpos(b, i), 0))
    fixed = lambda shape: pl.BlockSpec(shape, lambda b, i: (0, 0))
    if prepared:
        in_specs = [tok_spec(3 * DN_W), tok_spec(LANES), fixed((GROUP_W, GROUP_W)),
                    fixed((3 * LANES, 2 * DN_W))]
        args = [src, ab, ones_bd, esel]
    else:
        in_specs = [
            tok_spec(3 * DN_W),
            pl.BlockSpec((8, 3 * DN_W), lambda b, i: (jnp.maximum(blkpos(b, i) * sub - 1, 0), 0)),
            pl.BlockSpec((8, 3 * DN_W),
                         lambda b, i: (jnp.minimum((blkpos(b, i) + 1) * sub, ntok // 8 - 1), 0)),
            tok_spec(LANES), fixed((3, 3 * DN_W)), fixed((GROUP_W, GROUP_W)),
            fixed((3 * LANES, 2 * DN_W)),
        ]
        args = [src, src, src, ab, conv_w, ones_bd, esel]
    if has_state:
        in_specs.append(pl.BlockSpec((1, ngroups, GROUP_W, GROUP_W), lambda b, i: (b, 0, 0, 0)))
        args.append(s0_bd)
    out_specs = [tok_spec(DN_W)]
    out_shape = [jax.ShapeDtypeStruct((ntok, DN_W), F32)]
    if not prepared:
        out_specs.append(tok_spec(3 * DN_W))
        out_shape.append(jax.ShapeDtypeStruct((ntok, 3 * DN_W), F32))
    if want_state:
        out_specs.append(pl.BlockSpec((1, N_DN_HEADS, HEAD, HEAD), lambda b, i: (b, 0, 0, 0)))
        out_shape.append(jax.ShapeDtypeStruct((batch, N_DN_HEADS, HEAD, HEAD), F32))
    return pl.pallas_call(
        functools.partial(_dn_kernel, fwd=fwd, nblk=nblk, has_state=has_state,
                          want_state=want_state, prepared=prepared),
        grid=(batch, nblk),
        in_specs=in_specs,
        out_specs=out_specs,
        out_shape=out_shape,
        scratch_shapes=[pltpu.VMEM((ngroups, GROUP_W, GROUP_W), F32)],
        compiler_params=pltpu.CompilerParams(dimension_semantics=("arbitrary", "arbitrary"),
                                             vmem_limit_bytes=VMEM_LIMIT),
        name="deltanet_fwd" if fwd else "deltanet_bwd",
    )(*args)


def _post_kernel(*refs, final):
    (x_ref, oa_ref, of_ref, ob_ref, gout_ref, gates_ref, mod_ref, wpa_ref, wpd_ref, wout_ref,
     dng_ref, ones_ref, n2_ref, w1_ref, b1_ref, w2_ref, b2_ref) = refs[:17]
    if final:
        fn_ref, o_ref = refs[17:19]
    else:
        o_ref = refs[17]
    od = of_ref[...] + ob_ref[...]
    od = od * lax.rsqrt(_seg_sumsq(od, ones_ref[...]) * (1.0 / HEAD) + EPS) * dng_ref[...]
    gout = gout_ref[...]
    od = od * (gout * _sigmoid(gout))
    pa = jnp.dot(oa_ref[...], wpa_ref[...], preferred_element_type=F32)
    pd = jnp.dot(od.astype(BF16), wpd_ref[...], preferred_element_type=F32)
    merged = gates_ref[:, 0:D_MODEL] * pa + gates_ref[:, D_MODEL:2 * D_MODEL] * pd
    gate1 = mod_ref[0, :, 2 * D_MODEL:3 * D_MODEL]
    x = x_ref[...] + gate1 * jnp.dot(merged.astype(BF16), wout_ref[...],
                                     preferred_element_type=F32)

    shift2 = mod_ref[0, :, 3 * D_MODEL:4 * D_MODEL]
    scale2 = mod_ref[0, :, 4 * D_MODEL:5 * D_MODEL]
    gate2 = mod_ref[0, :, 5 * D_MODEL:6 * D_MODEL]
    h = _rms_rows(x) * n2_ref[...]
    h = h * (1.0 + scale2) + shift2
    a = jnp.dot(h.astype(BF16), w1_ref[...], preferred_element_type=F32) + b1_ref[...]
    a = jnp.square(jnp.maximum(a, 0.0))
    ff = jnp.dot(a.astype(BF16), w2_ref[...], preferred_element_type=F32) + b2_ref[...]
    out = x + gate2 * ff
    if final:
        out = _rms_rows(out) * fn_ref[...]
    o_ref[...] = out


def _post(x2d, oa, o_f, o_b, dcat, gates, mod, tiles_per_mod, wpa, wpd, wout, dng, ones_bd, n2, w1,
          b1, w2, b2, final_norm):
    ntok = x2d.shape[0]
    final = final_norm is not None
    row = lambda i: (i, 0)
    fixed = lambda shape: pl.BlockSpec(shape, lambda i: (0, 0), pipeline_mode=pl.Buffered(1))
    in_specs = [
        pl.BlockSpec((TOK_TILE, D_MODEL), row),
        pl.BlockSpec((TOK_TILE, ATTN_W), row),
        pl.BlockSpec((TOK_TILE, DN_W), row),
        pl.BlockSpec((TOK_TILE, DN_W), row),
        pl.BlockSpec((TOK_TILE, DN_W), lambda i: (i, 3)),
        pl.BlockSpec((TOK_TILE, 2 * D_MODEL), row),
        pl.BlockSpec((1, 1, 6 * D_MODEL), lambda i: (i // tiles_per_mod, 0, 0)),
        fixed((ATTN_W, D_MODEL)),
        fixed((DN_W, D_MODEL)),
        fixed((D_MODEL, D_MODEL)),
        fixed((1, DN_W)),
        fixed((GROUP_W, GROUP_W)),
        fixed((1, D_MODEL)),
        fixed((D_MODEL, D_FF)),
        fixed((1, D_FF)),
        fixed((D_FF, D_MODEL)),
        fixed((1, D_MODEL)),
    ]
    args = [x2d, oa, o_f, o_b, dcat, gates, mod, wpa, wpd, wout, dng, ones_bd, n2, w1, b1, w2, b2]
    if final:
        in_specs.append(fixed((1, D_MODEL)))
        args.append(final_norm)
    return pl.pallas_call(
        functools.partial(_post_kernel, final=final),
        grid=(ntok // TOK_TILE,),
        in_specs=in_specs,
        out_specs=pl.BlockSpec((TOK_TILE, D_MODEL), row),
        out_shape=jax.ShapeDtypeStruct((ntok, D_MODEL), F32),
        compiler_params=pltpu.CompilerParams(dimension_semantics=("arbitrary",),
                                             vmem_limit_bytes=VMEM_LIMIT),
        name="merge_mlp",
    )(*args)


def _rope_tables(n_tokens):
    freqs = HEAD // 4
    t = jnp.arange(n_tokens)
    row = (t // GRID_W).astype(F32)
    col = (t % GRID_W).astype(F32)
    inv = ROPE_THETA ** (-jnp.arange(freqs, dtype=F32) / freqs)
    ang = jnp.concatenate([row[:, None] * inv, col[:, None] * inv], axis=-1)
    cos, sin = jnp.cos(ang), jnp.sin(ang)
    cos_h = jnp.concatenate([cos, cos], axis=-1)
    sin_h = jnp.concatenate([-sin, sin], axis=-1)
    reps = LANES // HEAD
    return jnp.tile(cos_h, (1, reps)), jnp.tile(sin_h, (1, reps))


def _pack_w_in(w):
    pad = jnp.zeros((D_MODEL, LANES - 4 * N_DN_HEADS), w.dtype)
    return jnp.concatenate([w[:, :C_AB], w[:, C_AB:C_AB + 4 * N_DN_HEADS], pad,
                            w[:, C_AB + 4 * N_DN_HEADS:]], axis=1).astype(BF16)


def _pad_lanes(v):
    v = v.reshape(1, -1)
    return jnp.pad(v, ((0, 0), (0, LANES - v.shape[1])))


def _expand_selector(direction):
    src = jnp.arange(LANES)[:, None]
    dst = jnp.arange(2 * DN_W)[None, :]
    head = (dst % DN_W) // HEAD
    want = jnp.where(dst < DN_W, direction * N_DN_HEADS + head,
                     2 * N_DN_HEADS + direction * N_DN_HEADS + head)
    return jnp.tile((src == want).astype(BF16), (3, 1))


def _kv_head_orders(k, v):
    k2 = jnp.concatenate([k[:, 0], k[:, 1], k[:, 1], k[:, 0]], axis=-1)
    ones = jnp.ones(v.shape[:1] + v.shape[2:3] + (V_ROWS - HEAD,), v.dtype)
    vt = jnp.concatenate([v[:, 0], ones, v[:, 1], ones], axis=-1)
    return k2.astype(BF16), vt.transpose(0, 2, 1).astype(BF16)


def _state_to_block_diag(s):
    b = s.shape[0]
    s = s.reshape(b, DN_W // GROUP_W, SEGS, HEAD, HEAD).astype(F32)
    eye = jnp.eye(SEGS, dtype=F32)
    bd = s[:, :, :, :, None, :] * eye[None, None, :, None, :, None]
    return bd.reshape(b, DN_W // GROUP_W, GROUP_W, GROUP_W)


def kernel(x_prompt, x_sample, c, cache_k, cache_v, state_delta, c_ctx, w_mod, b_mod, norm1, norm2,
           w_in, conv_w, q_gain, k_gain, a_log, dt_bias, dn_gain, w_pa, w_pd, w_out, w1, b1, w2, b2,
           final_norm):
    depth = w_in.shape[0]
    batch, seq, _ = x_prompt.shape
    dec_batch, dec_seq, _ = x_sample.shape
    assert seq == TOK_TILE and dec_seq % TOK_TILE == 0

    mod_rows = 8
    cvec = jnp.concatenate([c, c_ctx[None, :],
                            jnp.zeros((mod_rows - dec_batch - 1, D_MODEL), F32)], axis=0)
    mod_all = _modulation(cvec, w_mod.astype(BF16), b_mod)

    ones_bd = (jnp.arange(GROUP_W)[:, None] // HEAD == jnp.arange(GROUP_W)[None, :] // HEAD
               ).astype(BF16)
    esel = [_expand_selector(0), _expand_selector(1)]
    rope_tabs = _rope_tables(dec_seq)

    xp = x_prompt.reshape(batch * seq, D_MODEL)
    xs = x_sample.reshape(dec_batch * dec_seq, D_MODEL)
    ks_out, vs_out, st_out = [], [], []
    for l in range(depth):
        w_cat = _pack_w_in(w_in[l])
        wvt = w_in[l][:, ATTN_W + KV_W:ATTN_W + 2 * KV_W].T.astype(BF16)
        n1 =norm1[l].reshape(1, D_MODEL)
        n2 = norm2[l].reshape(1, D_MODEL)
        qg = jnp.tile(q_gain[l], N_Q_HEADS).reshape(1, ATTN_W)
        kg = jnp.tile(k_gain[l], N_KV_HEADS).reshape(1, KV_W)
        dng = jnp.tile(dn_gain[l], N_DN_HEADS).reshape(1, DN_W)
        alog = _pad_lanes(a_log[l])
        dtb = _pad_lanes(dt_bias[l])
        wpa, wpd, wout = w_pa[l].astype(BF16), w_pd[l].astype(BF16), w_out[l].astype(BF16)
        w1b, w2b = w1[l].astype(BF16), w2[l].astype(BF16)
        b1r, b2r = b1[l].reshape(1, D_FF), b2[l].reshape(1, D_MODEL)
        last = l == depth - 1
        fin = final_norm.reshape(1, D_MODEL) if last else None
        mod_lat = mod_all[l, :dec_batch].reshape(dec_batch, 1, 6 * D_MODEL)
        mod_ctx = mod_all[l, dec_batch:dec_batch + 1].reshape(1, 1, 6 * D_MODEL)

        tpm = batch * seq // TOK_TILE
        q, k2, vt, dcat, ab, gates, k_c, v_c = _inproj(xp, mod_ctx, tpm, n1, w_cat, wvt, qg, kg,
                                                       alog, dtb, ones_bd, None, True, seq)
        oa = _attention(q, k2, vt, None, None, batch, seq)
        o_f, qkv, s_f = _deltanet(dcat, ab, conv_w[l], ones_bd, esel[0], None, batch, seq, True, True)
        o_b, s_b = _deltanet(qkv, ab, None, ones_bd, esel[1], None, batch, seq, False, True)
        xp = _post(xp, oa, o_f, o_b, dcat, gates, mod_ctx, tpm, wpa, wpd, wout, dng, ones_bd, n2,
                   w1b, b1r, w2b, b2r, fin)
        ks_out.append(k_c)
        vs_out.append(v_c)
        st_out.append(jnp.stack([s_f, s_b], axis=1))

        tpm = dec_seq // TOK_TILE
        k2c, vtc = _kv_head_orders(cache_k[:, l], cache_v[:, l])
        q, k2, vt, dcat, ab, gates = _inproj(xs, mod_lat, tpm, n1, w_cat, wvt, qg, kg, alog, dtb,
                                             ones_bd, rope_tabs, False, dec_seq)
        oa = _attention(q, k2, vt, k2c, vtc, dec_batch, dec_seq)
        s0f = _state_to_block_diag(state_delta[:, l, 0])
        s0b = _state_to_block_diag(state_delta[:, l, 1])
        o_f, qkv = _deltanet(dcat, ab, conv_w[l], ones_bd, esel[0], s0f, dec_batch, dec_seq, True,
                             False)
        (o_b,) = _deltanet(qkv, ab, None, ones_bd, esel[1], s0b, dec_batch, dec_seq, False, False)
        xs = _post(xs, oa, o_f, o_b, dcat, gates, mod_lat, tpm, wpa, wpd, wout, dng, ones_bd, n2,
                   w1b, b1r, w2b, b2r, fin)

    y_prompt = xp.reshape(batch, seq, D_MODEL)
    y_sample = xs.reshape(dec_batch, dec_seq, D_MODEL)
    return (y_prompt, y_sample, jnp.stack(ks_out, axis=1), jnp.stack(vs_out, axis=1),
            jnp.stack(st_out, axis=1))
```

```python
import functools

import jax
import jax.numpy as jnp
from jax import lax
from jax.experimental import pallas as pl
from jax.experimental.pallas import tpu as pltpu

F32 = jnp.float32
BF16 = jnp.bfloat16

D_MODEL = 1024
HEAD = 64
N_Q_HEADS = 8
N_KV_HEADS = 2
N_DN_HEADS = 8
ATTN_W = N_Q_HEADS * HEAD
KV_W = N_KV_HEADS * HEAD
DN_W = N_DN_HEADS * HEAD
CHUNK = 64
GRID_W = 64
D_FF = 4 * D_MODEL
ROPE_THETA = 10000.0
EPS = 1e-6
LOG2_E = 1.4426950408889634

LANES = 128
GROUP_W = 256
SEGS = GROUP_W // HEAD
TOK_TILE = 256
V_ROWS = HEAD + 16
VT_ROWS = N_KV_HEADS * V_ROWS
ATTN_CHUNK_TILES = 2
ATTN_Q_TILE = 512
DN_TILE = 1024
INPROJ_TILE = 512
DN_RUNS = 4
DN_MIN_RUN = 2
VMEM_LIMIT = 56 * 1024 * 1024

C_A = 0
C_D = ATTN_W + 2 * KV_W
C_AB = C_D + 4 * DN_W
C_G = C_AB + LANES
C_END = C_G + 2 * D_MODEL


def _sigmoid(x):
    return 1.0 / (1.0 + jnp.exp(-x))


def _bdot(a, b):
    return jnp.dot(a.astype(BF16), b.astype(BF16), preferred_element_type=F32)


def _split2(x):
    hi = x.astype(BF16)
    return hi, (x - hi.astype(F32)).astype(BF16)


def _split3(x):
    hi = x.astype(BF16)
    r = x - hi.astype(F32)
    mid = r.astype(BF16)
    return hi, mid, (r - mid.astype(F32)).astype(BF16)


def _sel_dot_left(x, sel3):
    return jnp.dot(jnp.concatenate(_split3(x), axis=1), sel3, preferred_element_type=F32)


def _sel_dot_right(sel3, x):
    return jnp.dot(sel3, jnp.concatenate(_split3(x), axis=0), preferred_element_type=F32)


def _seg_sumsq(x, ones_bd):
    x2 = x * x
    hi = x2.astype(BF16)
    lo = (x2 - hi.astype(F32)).astype(BF16)
    width = x.shape[1]
    if width == LANES:
        m = ones_bd[:LANES, :LANES]
        return (jnp.dot(hi, m, preferred_element_type=F32)
                + jnp.dot(lo, m, preferred_element_type=F32))
    outs = []
    for g in range(width // GROUP_W):
        sl = slice(g * GROUP_W, (g + 1) * GROUP_W)
        outs.append(jnp.dot(hi[:, sl], ones_bd, preferred_element_type=F32)
                    + jnp.dot(lo[:, sl], ones_bd, preferred_element_type=F32))
    return jnp.concatenate(outs, axis=1)


def _rms_rows(x):
    return x * lax.rsqrt(jnp.mean(x * x, axis=-1, keepdims=True) + EPS)


def _mod_kernel(c_ref, w_ref, b_ref, o_ref):
    c = c_ref[...]
    s = c * _sigmoid(c)
    o_ref[0] = jnp.dot(s.astype(BF16), w_ref[0], preferred_element_type=F32) + b_ref[0]


def _modulation(cvec, w_mod_bf, b_mod):
    depth = w_mod_bf.shape[0]
    rows = cvec.shape[0]
    nblk = 6 * D_MODEL // D_MODEL
    return pl.pallas_call(
        _mod_kernel,
        grid=(depth, nblk),
        in_specs=[
            pl.BlockSpec((rows, D_MODEL), lambda l, j: (0, 0)),
            pl.BlockSpec((1, D_MODEL, D_MODEL), lambda l, j: (l, 0, j)),
            pl.BlockSpec((1, 1, D_MODEL), lambda l, j: (l, 0, j)),
        ],
        out_specs=pl.BlockSpec((1, rows, D_MODEL), lambda l, j: (l, 0, j)),
        out_shape=jax.ShapeDtypeStruct((depth, rows, 6 * D_MODEL), F32),
        name="adaln_mod",
    )(cvec, w_mod_bf, b_mod.reshape(depth, 1, 6 * D_MODEL))


def _rope(x, cos, sin_signed):
    width = x.shape[1]
    lane = lax.broadcasted_iota(jnp.int32, x.shape, 1)
    first_half = (lane % HEAD) < (HEAD // 2)
    rot = jnp.where(first_half, pltpu.roll(x, width - HEAD // 2, 1), pltpu.roll(x, HEAD // 2, 1))
    return x * cos + rot * sin_signed


def _inproj_kernel(*refs, rope, cache_out):
    (x_ref, mod_ref, n1_ref, w_ref, wvt_ref, qg_ref, kg_ref, alog_ref, dtb_ref,
     ones_ref) = refs[:10]
    pos = 10
    if rope:
        cos_ref, sin_ref = refs[pos:pos + 2]
        pos += 2
    q_ref, k2_ref, vt_ref, d_ref, ab_ref, g_ref = refs[pos:pos + 6]
    pos += 6
    if cache_out:
        kc_ref, vc_ref = refs[pos:pos + 2]

    shift1 = mod_ref[0, :, 0:D_MODEL]
    scale1 = mod_ref[0, :, D_MODEL:2 * D_MODEL]
    h = _rms_rows(x_ref[...]) * n1_ref[...]
    h = h * (1.0 + scale1) + shift1
    h16 = h.astype(BF16)
    y = jnp.dot(h16, w_ref[...], preferred_element_type=F32)
    ones_bd = ones_ref[...]

    vt = lax.dot_general(wvt_ref[...], h16, (((1,), (1,)), ((), ())),
                         preferred_element_type=F32).astype(BF16)
    subtiles = [slice(t * TOK_TILE, (t + 1) * TOK_TILE) for t in range(vt.shape[1] // TOK_TILE)]
    ones = jnp.ones((V_ROWS - HEAD, TOK_TILE), BF16)
    for t, ts in enumerate(subtiles):
        for kv in range(N_KV_HEADS):
            vt_ref[t, kv * V_ROWS:kv * V_ROWS + HEAD, :] = vt[kv * HEAD:(kv + 1) * HEAD, ts]
            vt_ref[t, kv * V_ROWS + HEAD:(kv + 1) * V_ROWS, :] = ones

    qa = y[:, C_A:C_A + ATTN_W]
    qn = qa * lax.rsqrt(_seg_sumsq(qa, ones_bd) * (1.0 / HEAD) + EPS) * qg_ref[...]
    ka = y[:, C_A + ATTN_W:C_A + ATTN_W + KV_W]
    kn = ka * lax.rsqrt(_seg_sumsq(ka, ones_bd) * (1.0 / HEAD) + EPS) * kg_ref[...]
    va = y[:, C_A + ATTN_W + KV_W:C_D]
    if cache_out:
        for t, ts in enumerate(subtiles):
            kc_ref[t, 0] = kn[ts, :HEAD]
            kc_ref[t, 1] = kn[ts, HEAD:]
            vc_ref[t, 0] = va[ts, :HEAD]
            vc_ref[t, 1] = va[ts, HEAD:]
    if rope:
        cos = cos_ref[...]
        sin = sin_ref[...]
        qn = _rope(qn, jnp.concatenate([cos] * (ATTN_W // LANES), axis=1),
                   jnp.concatenate([sin] * (ATTN_W // LANES), axis=1))
        kn = _rope(kn, cos, sin)
    q_ref[...] = (qn * (HEAD ** -0.5 * LOG2_E)).astype(BF16)
    kn16 = kn.astype(BF16)
    kn16_swapped = pltpu.roll(kn, HEAD, 1).astype(BF16)
    for t, ts in enumerate(subtiles):
        k2_ref[t, :, 0:LANES] = kn16[ts]
        k2_ref[t, :, LANES:2 * LANES] = kn16_swapped[ts]

    d_ref[...] = y[:, C_D:C_AB]

    z = y[:, C_AB:C_G]
    zb = z + dtb_ref[...]
    softplus = jnp.maximum(zb, 0.0) + jnp.log(1.0 + jnp.exp(-jnp.abs(zb)))
    log_a = -jnp.exp(alog_ref[...]) * softplus
    lane = lax.broadcasted_iota(jnp.int32, z.shape, 1)
    ab_ref[...] = jnp.where(lane < 2 * N_DN_HEADS, log_a,
                            jnp.where(lane < 4 * N_DN_HEADS, _sigmoid(z), 0.0))

    g_ref[...] = _sigmoid(y[:, C_G:C_END]).astype(BF16)


def _inproj(x2d, mod, tiles_per_mod, n1, w_cat, wvt, qg, kg, alog, dtb, ones_bd, rope_tabs,
            cache_out, seq_len):
    ntok = x2d.shape[0]
    ntiles = ntok // TOK_TILE
    tile = INPROJ_TILE
    sub = tile // TOK_TILE
    steps_per_mod = tiles_per_mod // sub
    rope = rope_tabs is not None
    const = lambda shape: pl.BlockSpec(shape, lambda i: (0, 0), pipeline_mode=pl.Buffered(1))
    row = lambda i: (i, 0)
    in_specs = [
        pl.BlockSpec((tile, D_MODEL), row),
        pl.BlockSpec((1, 1, 6 * D_MODEL), lambda i: (i // steps_per_mod, 0, 0)),
        const((1, D_MODEL)),
        const((D_MODEL, C_END)),
        const((KV_W, D_MODEL)),
        const((1, ATTN_W)),
        const((1, KV_W)),
        const((1, LANES)),
        const((1, LANES)),
        const((GROUP_W, GROUP_W)),
    ]
    args = [x2d, mod, n1, w_cat, wvt, qg, kg, alog, dtb, ones_bd]
    if rope:
        steps_per_seq = seq_len // tile
        in_specs += [pl.BlockSpec((tile, LANES), lambda i: (i % steps_per_seq, 0))] * 2
        args += list(rope_tabs)
    out_specs = [
        pl.BlockSpec((tile, ATTN_W), row),
        pl.BlockSpec((sub, TOK_TILE, 2 * LANES), lambda i: (i, 0, 0)),
        pl.BlockSpec((sub, VT_ROWS, TOK_TILE), lambda i: (i, 0, 0)),
        pl.BlockSpec((tile, 4 * DN_W), row),
        pl.BlockSpec((tile, LANES), row),
        pl.BlockSpec((tile, 2 * D_MODEL), row),
    ]
    out_shape = [
        jax.ShapeDtypeStruct((ntok, ATTN_W), BF16),
        jax.ShapeDtypeStruct((ntiles, TOK_TILE, 2 * LANES), BF16),
        jax.ShapeDtypeStruct((ntiles, VT_ROWS, TOK_TILE), BF16),
        jax.ShapeDtypeStruct((ntok, 4 * DN_W), F32),
        jax.ShapeDtypeStruct((ntok, LANES), F32),
        jax.ShapeDtypeStruct((ntok, 2 * D_MODEL), BF16),
    ]
    if cache_out:
        assert seq_len == TOK_TILE
        out_specs += [pl.BlockSpec((sub, N_KV_HEADS, TOK_TILE, HEAD), lambda i: (i, 0, 0, 0))] * 2
        out_shape += [jax.ShapeDtypeStruct((ntiles, N_KV_HEADS, TOK_TILE, HEAD), F32)] * 2
    return pl.pallas_call(
        functools.partial(_inproj_kernel, rope=rope, cache_out=cache_out),
        grid=(ntok // tile,),
        in_specs=in_specs,
        out_specs=out_specs,
        out_shape=out_shape,
        compiler_params=pltpu.CompilerParams(dimension_semantics=("arbitrary",),
                                             vmem_limit_bytes=VMEM_LIMIT),
        name="inproj",
    )(*args)


def _attn_kernel(*refs, has_ctx):
    if has_ctx:
        q_ref, k2_ref, vt_ref, k2c_ref, vtc_ref, o_ref, s_scr, p_scr, acc_scr = refs
    else:
        q_ref, k2_ref, vt_ref, o_ref, s_scr, p_scr, acc_scr = refs
    tq = q_ref.shape[0]
    lane = lax.broadcasted_iota(jnp.int32, (1, LANES), 1)
    lower = lane < HEAD
    nt = (((1,), (1,)), ((), ()))
    qh, sel, vrow = [], [], []
    for head in range(N_Q_HEADS):
        j, half = divmod(head, 2)
        kv = head // (N_Q_HEADS // N_KV_HEADS)
        qb = q_ref[:, j * LANES:(j + 1) * LANES]
        qh.append(jnp.where(lower if half == 0 else jnp.logical_not(lower), qb, jnp.zeros_like(qb)))
        sel.append(0 if half == kv else 1)
        vrow.append(kv * V_ROWS)

    heads = range(N_Q_HEADS)
    tiles = s_scr.shape[2] // TOK_TILE
    nchunks = k2_ref.shape[0] // tiles

    def is_ctx(c):
        return has_ctx and isinstance(c, int) and c < 0

    def key_tiles(c):
        if is_ctx(c):
            return [(k2c_ref[0], vtc_ref[0])]
        return [(k2_ref[c * tiles + t], vt_ref[c * tiles + t]) for t in range(tiles)]

    def put_scores(c, slot):
        k_chunk = jnp.concatenate([kt for kt, _ in key_tiles(c)], axis=0)
        for h in heads:
            s_scr[slot, h, 0:k_chunk.shape[0]] = lax.dot_general(
                k_chunk[:, sel[h] * LANES:(sel[h] + 1) * LANES], qh[h], nt,
                preferred_element_type=F32)

    def add_values(c, slot, alpha):
        vts = [vt for _, vt in key_tiles(c)]
        for h in heads:
            pv = alpha[h] * acc_scr[h]
            for t, vt_tile in enumerate(vts):
                pv = pv + jnp.dot(vt_tile[vrow[h]:vrow[h] + V_ROWS, :],
                                  p_scr[slot, h, t * TOK_TILE:(t + 1) * TOK_TILE],
                                  preferred_element_type=F32)
            acc_scr[h] = pv

    def softmax(c, slot, stats):
        rows = k2c_ref.shape[1] if is_ctx(c) else tiles * TOK_TILE
        new_stats, alpha = [], []
        for h in heads:
            sx = s_scr[slot, h, 0:rows]
            m_new = jnp.maximum(stats[h], jnp.max(sx, axis=0, keepdims=True))
            p_scr[slot, h, 0:rows] = jnp.exp2(sx - m_new).astype(BF16)
            alpha.append(jnp.exp2(stats[h] - m_new))
            new_stats.append(m_new)
        return tuple(new_stats), tuple(alpha)

    def step(c, parity, stats, alpha, has_prev, has_next):
        if has_next:
            put_scores(c + 1, 1 - parity)
        if has_prev:
            add_values(c - 1, 1 - parity, alpha)
        return softmax(c, parity, stats)

    first = -1 if has_ctx else 0
    last = nchunks - 1
    stats = tuple(jnp.full((1, tq), -1e30, F32) for _ in heads)
    alpha = tuple(jnp.zeros((1, tq), F32) for _ in heads)
    acc_scr[...] = jnp.zeros_like(acc_scr)
    put_scores(first, first % 2)
    c = first
    while c <= min(0, last):
        stats, alpha = step(c, c % 2, stats, alpha, c > first, c < last)
        c += 1
    per_trip = 2
    ntrips = max(last - c, 0) // per_trip

    def body(i, carry):
        c0 = c + per_trip * i
        for j in range(per_trip):
            carry = step(c0 + j, (c + j) % 2, *carry, True, True)
        return carry

    if ntrips > 0:
        stats, alpha = lax.fori_loop(0, ntrips, body, (stats, alpha))
        c += per_trip * ntrips
    while c <= last:
        stats, alpha = step(c, c % 2, stats, alpha, c > first, c < last)
        c += 1
    add_values(last, last % 2, alpha)
    res = []
    for h in heads:
        acc = acc_scr[h]
        res.append(acc[0:HEAD, :] / acc[HEAD:HEAD + 1, :])
    blocks = [jnp.concatenate(res[2 * j:2 * j + 2], axis=0).T for j in range(ATTN_W // LANES)]
    o_ref[...] = jnp.concatenate(blocks, axis=1).astype(BF16)


def _attention(q, k2, vt, k2c, vtc, batch, seq_len):
    ntok = q.shape[0]
    nkt = seq_len // TOK_TILE
    has_ctx = k2c is not None
    chunk_keys = ATTN_CHUNK_TILES * TOK_TILE if nkt % ATTN_CHUNK_TILES == 0 else TOK_TILE
    tq = min(ATTN_Q_TILE, seq_len)
    nq = seq_len // tq
    in_specs = [
        pl.BlockSpec((tq, ATTN_W), lambda b, i: (b * nq + i, 0)),
        pl.BlockSpec((nkt, TOK_TILE, 2 * LANES), lambda b, i: (b, 0, 0)),
        pl.BlockSpec((nkt, VT_ROWS, TOK_TILE), lambda b, i: (b, 0, 0)),
    ]
    args = [q, k2, vt]
    if has_ctx:
        in_specs += [pl.BlockSpec((1,) + k2c.shape[1:], lambda b, i: (b, 0, 0)),
                     pl.BlockSpec((1,) + vtc.shape[1:], lambda b, i: (b, 0, 0))]
        args += [k2c, vtc]
    return pl.pallas_call(
        functools.partial(_attn_kernel, has_ctx=has_ctx),
        grid=(batch, nq),
        in_specs=in_specs,
        out_specs=pl.BlockSpec((tq, ATTN_W), lambda b, i: (b * nq + i, 0)),
        out_shape=jax.ShapeDtypeStruct((ntok, ATTN_W), BF16),
        scratch_shapes=[pltpu.VMEM((2, N_Q_HEADS, chunk_keys, tq), F32),
                        pltpu.VMEM((2, N_Q_HEADS, chunk_keys, tq), BF16),
                        pltpu.VMEM((N_Q_HEADS, V_ROWS, tq), F32)],
        compiler_params=pltpu.CompilerParams(dimension_semantics=("arbitrary", "arbitrary"),
                                             vmem_limit_bytes=VMEM_LIMIT),
        name="attention",
    )(*args)


def _block_diag(x):
    seg = lax.broadcasted_iota(jnp.int32, x.shape, 1) // HEAD
    return jnp.concatenate([jnp.where(seg == s, x, jnp.zeros_like(x)) for s in range(SEGS)], axis=0)


def _dn_kernel(*refs, fwd, nblk, has_state, want_state, prepared):
    if prepared:
        qkv_ref, ab_ref, ones_ref, esel_ref = refs[:4]
        pos = 4
    else:
        x_ref, xp_ref, xn_ref, ab_ref, cw_ref, ones_ref, esel_ref = refs[:7]
        pos = 7
    if has_state:
        s0_ref = refs[pos]
        pos += 1
    o_ref = refs[pos]
    pos += 1
    if not prepared:
        qkv_out_ref = refs[pos]
        pos += 1
    if want_state:
        sf_ref = refs[pos]
        pos += 1
    s_scr = refs[pos]

    tb = o_ref.shape[0]
    ngroups = DN_W // GROUP_W
    blk = pl.program_id(1)
    rb = blk if fwd else nblk - 1 - blk

    @pl.when(blk == 0)
    def _():
        if has_state:
            s_scr[...] = s0_ref[0]
        else:
            s_scr[...] = jnp.zeros_like(s_scr)

    ones_bd = ones_ref[...]
    if prepared:
        q = qkv_ref[:, 0:DN_W]
        k = qkv_ref[:, DN_W:2 * DN_W]
        v = qkv_ref[:, 2 * DN_W:3 * DN_W]
    else:
        x = x_ref[...]
        rows = lax.broadcasted_iota(jnp.int32, (tb, 1), 0)
        prev = jnp.where(rb > 0, xp_ref[7:8, :], 0.0)
        nxt = jnp.where(rb < nblk - 1, xn_ref[0:1, :], 0.0)
        xm1 = jnp.where(rows == 0, prev, pltpu.roll(x, 1, 0))
        xp1 = jnp.where(rows == tb - 1, nxt, pltpu.roll(x, tb - 1, 0))
        y = xm1 * cw_ref[0:1, :] + x * cw_ref[1:2, :] + xp1 * cw_ref[2:3, :]
        y = y * _sigmoid(y)
        q = y[:, 0:DN_W]
        k = y[:, DN_W:2 * DN_W]
        v = y[:, 2 * DN_W:3 * DN_W]
        q = q * lax.rsqrt(_seg_sumsq(q, ones_bd) + EPS) * (HEAD ** -0.5)
        k = k * lax.rsqrt(_seg_sumsq(k, ones_bd) + EPS)
        qkv_out_ref[:, 0:DN_W] = q
        qkv_out_ref[:, DN_W:2 * DN_W] = k
        qkv_out_ref[:, 2 * DN_W:3 * DN_W] = v

    ex = _sel_dot_left(ab_ref[...], esel_ref[...])
    la = ex[:, 0:DN_W]
    be = ex[:, DN_W:2 * DN_W]
    same_chunk = ones_bd.astype(F32)
    r_i = lax.broadcasted_iota(jnp.int32, (GROUP_W, GROUP_W), 0) % CHUNK
    c_i = lax.broadcasted_iota(jnp.int32, (GROUP_W, GROUP_W), 1) % CHUNK
    tri = jnp.where((c_i <= r_i) if fwd else (c_i >= r_i), ones_bd, jnp.zeros_like(ones_bd))
    tri3 = jnp.concatenate([tri] * 3, axis=1)
    ones3 = jnp.concatenate([ones_bd] * 3, axis=1)
    t_i = lax.broadcasted_iota(jnp.int32, (GROUP_W, DN_W), 0) % CHUNK
    l_i = lax.broadcasted_iota(jnp.int32, (GROUP_W, DN_W), 1) % CHUNK
    row_mask = (t_i <= l_i) if fwd else (t_i >= l_i)
    la_parts = [la[r:r + GROUP_W] for r in range(0, tb, GROUP_W)]
    g_col = jnp.concatenate([_sel_dot_right(tri3, part) for part in la_parts], axis=0)
    g_row = jnp.concatenate([_sel_dot_right(ones3, jnp.where(row_mask, part, 0.0))
                             for part in la_parts], axis=0)

    ii = lax.broadcasted_iota(jnp.int32, (CHUNK, GROUP_W), 0)
    jj = lax.broadcasted_iota(jnp.int32, (CHUNK, GROUP_W), 1) % HEAD
    incl = (jj <= ii) if fwd else (jj >= ii)
    strict = (jj < ii) if fwd else (jj > ii)
    eye = jnp.where(jj == ii, 1.0, 0.0)
    nt = (((1,), (1,)), ((), ()))
    tn = (((0,), (0,)), ((), ()))
    nchunk = tb // CHUNK

    order = [c if fwd else nchunk - 1 - c for c in range(nchunk)]
    dot = functools.partial(jnp.dot, preferred_element_type=F32)
    n_doublings = CHUNK.bit_length() - 3
    assert 2 ** (n_doublings + 2) == CHUNK

    def precompute(chunks, out):
        pairs = [(slice(c * CHUNK, (c + 1) * CHUNK), slice(g * GROUP_W, (g + 1) * GROUP_W), g)
                 for c in chunks for g in range(ngroups)]
        dec, kq = [], []
        for rs, ls, _ in pairs:
            dec.append(jnp.exp(jnp.where(incl, g_col[rs, ls] - g_row[rs, ls], -1e30)))
            kc16 = k[rs, ls].astype(BF16)
            kq.append(lax.dot_general(jnp.concatenate([kc16, q[rs, ls].astype(BF16)], axis=0),
                                      _block_diag(kc16), nt, preferred_element_type=F32))
        yield
        neg_l = [-jnp.where(strict, be[rs, ls] * kqp[:CHUNK] * d, 0.0)
                 for (rs, ls, _), kqp, d in zip(pairs, kq, dec)]
        acc = [eye + n for n in neg_l]
        power = list(neg_l)
        for it in range(n_doublings + 1):
            for p in range(len(pairs)):
                wmat = _block_diag(power[p].astype(BF16))
                if it == 0:
                    power[p] = dot(power[p].astype(BF16), wmat)
                elif it < n_doublings:
                    r = dot(jnp.concatenate([acc[p], power[p]], axis=0).astype(BF16), wmat)
                    acc[p] = acc[p] + r[:CHUNK]
                    power[p] = r[CHUNK:]
                else:
                    acc[p] = acc[p] + dot(acc[p].astype(BF16), wmat)
            yield
        xs = [_split2(a) for a in acc]
        ns = [_split2(n) for n in neg_l]
        nx1 = [dot(jnp.concatenate([nh, nl], axis=0), _block_diag(xh))
               for (nh, nl), (xh, _) in zip(ns, xs)]
        nx2 = [dot(nh, _block_diag(xl)) for (nh, _), (_, xl) in zip(ns, xs)]
        yield
        resid = [eye - a + (n1[:CHUNK] + n1[CHUNK:] + n2) for a, n1, n2 in zip(acc, nx1, nx2)]
        acc = [a + dot(xh, _block_diag(rsd.astype(BF16)))
               for a, (xh, _), rsd in zip(acc, xs, resid)]
        yield
        eg_c = [jnp.exp(g_col[rs, ls]) for rs, ls, _ in pairs]
        wu = []
        for (rs, ls, _), a, eg in zip(pairs, acc, eg_c):
            rw = (be[rs, ls] * eg * k[rs, ls]).astype(BF16)
            ru = (be[rs, ls] * v[rs, ls]).astype(BF16)
            wu.append(dot(a.astype(BF16),
                          jnp.concatenate([_block_diag(rw), _block_diag(ru)], axis=1)))
        yield
        for (rs, ls, g), kqp, d, eg, wup in zip(pairs, kq, dec, eg_c, wu):
            gc = g_col[rs, ls]
            g_last = gc[CHUNK - 1:CHUNK, :] if fwd else gc[0:1, :]
            out.append(dict(
                rs=rs, ls=ls, g=g, u=wup[:, GROUP_W:],
                lhs=jnp.concatenate([wup[:, :GROUP_W], q[rs, ls] * eg], axis=0).astype(BF16),
                qk_d=(kqp[CHUNK:] * d).astype(BF16),
                k_dec=(k[rs, ls] * jnp.exp(g_last - gc)).astype(BF16),
                eg_last=jnp.exp(g_last)))

    def scan(items):
        for i in range(0, len(items), ngroups):
            grp = items[i:i + ngroups]
            s_bd = [s_scr[it["g"]] for it in grp]
            r = [dot(it["lhs"], s.astype(BF16)) for it, s in zip(grp, s_bd)]
            yield
            u16 = [(it["u"] - rp[:CHUNK]).astype(BF16) for it, rp in zip(grp, r)]
            o_c = [rp[CHUNK:] + dot(it["qk_d"], _block_diag(u)) for it, rp, u in zip(grp, r, u16)]
            upd = [lax.dot_general(it["k_dec"], u, tn, preferred_element_type=F32)
                   for it, u in zip(grp, u16)]
            for it, s, o, up in zip(grp, s_bd, o_c, upd):
                s_scr[it["g"]] = s * it["eg_last"] + same_chunk * up
                o_ref[it["rs"], it["ls"]] = o
            yield

    run = nchunk // DN_RUNS if nchunk >= DN_RUNS * DN_MIN_RUN else nchunk
    runs = [order[i:i + run] for i in range(0, nchunk, run)]
    pending = None
    for chunks in runs:
        items = []
        stages = precompute(chunks, items)
        steps = scan(pending) if pending is not None else iter(())
        for _ in stages:
            next(steps, None)
        for _ in steps:
            pass
        pending = items
    for _ in scan(pending):
        pass

    if want_state:
        @pl.when(blk == nblk - 1)
        def _():
            for g in range(ngroups):
                for s in range(SEGS):
                    rows = s_scr[g, s * HEAD:(s + 1) * HEAD, :]
                    sf_ref[0, g * SEGS + s] = rows[:, s * HEAD:(s + 1) * HEAD]


def _deltanet(src, ab, conv_w, ones_bd, esel, s0_bd, batch, seq_len, fwd, want_state):
    ntok = src.shape[0]
    tb = min(DN_TILE, seq_len)
    nblk = seq_len // tb
    has_state = s0_bd is not None
    prepared = conv_w is None
    ngroups = DN_W // GROUP_W
    sub = tb // 8

    def blkpos(b, i):
        return b * nblk + (i if fwd else nblk - 1 - i)

    tok_spec = lambda width: pl.BlockSpec((tb, width), lambda b, i: (blkpos(b, i), 0))
    fixed = lambda shape: pl.BlockSpec(shape, lambda b, i: (0, 0))
    if prepared:
        in_specs = [tok_spec(3 * DN_W), tok_spec(LANES), fixed((GROUP_W, GROUP_W)),
                    fixed((3 * LANES, 2 * DN_W))]
        args = [src, ab, ones_bd, esel]
    else:
        in_specs = [
            tok_spec(3 * DN_W),
            pl.BlockSpec((8, 3 * DN_W), lambda b, i: (jnp.maximum(blkpos(b, i) * sub - 1, 0), 0)),
            pl.BlockSpec((8, 3 * DN_W),
                         lambda b, i: (jnp.minimum((blkpos(b, i) + 1) * sub, ntok // 8 - 1), 0)),
            tok_spec(LANES), fixed((3, 3 * DN_W)), fixed((GROUP_W, GROUP_W)),
            fixed((3 * LANES, 2 * DN_W)),
        ]
        args = [src, src, src, ab, conv_w, ones_bd, esel]
    if has_state:
        in_specs.append(pl.BlockSpec((1, ngroups, GROUP_W, GROUP_W), lambda b, i: (b, 0, 0, 0)))
        args.append(s0_bd)
    out_specs = [tok_spec(DN_W)]
    out_shape = [jax.ShapeDtypeStruct((ntok, DN_W), F32)]
    if not prepared:
        out_specs.append(tok_spec(3 * DN_W))
        out_shape.append(jax.ShapeDtypeStruct((ntok, 3 * DN_W), F32))
    if want_state:
        out_specs.append(pl.BlockSpec((1, N_DN_HEADS, HEAD, HEAD), lambda b, i: (b, 0, 0, 0)))
        out_shape.append(jax.ShapeDtypeStruct((batch, N_DN_HEADS, HEAD, HEAD), F32))
    return pl.pallas_call(
        functools.partial(_dn_kernel, fwd=fwd, nblk=nblk, has_state=has_state,
                          want_state=want_state, prepared=prepared),
        grid=(batch, nblk),
        in_specs=in_specs,
        out_specs=out_specs,
        out_shape=out_shape,
        scratch_shapes=[pltpu.VMEM((ngroups, GROUP_W, GROUP_W), F32)],
        compiler_params=pltpu.CompilerParams(dimension_semantics=("arbitrary", "arbitrary"),
                                             vmem_limit_bytes=VMEM_LIMIT),
        name="deltanet_fwd" if fwd else "deltanet_bwd",
    )(*args)


def _post_kernel(*refs, final):
    (x_ref, oa_ref, of_ref, ob_ref, gout_ref, gates_ref, mod_ref, wpa_ref, wpd_ref, wout_ref,
     dng_ref, ones_ref, n2_ref, w1_ref, b1_ref, w2_ref, b2_ref) = refs[:17]
    if final:
        fn_ref, o_ref = refs[17:19]
    else:
        o_ref = refs[17]
    od = of_ref[...] + ob_ref[...]
    od = od * lax.rsqrt(_seg_sumsq(od, ones_ref[...]) * (1.0 / HEAD) + EPS) * dng_ref[...]
    gout = gout_ref[...]
    od = od * (gout * _sigmoid(gout))
    pa = jnp.dot(oa_ref[...], wpa_ref[...], preferred_element_type=F32)
    pd = jnp.dot(od.astype(BF16), wpd_ref[...], preferred_element_type=F32)
    merged = gates_ref[:, 0:D_MODEL] * pa + gates_ref[:, D_MODEL:2 * D_MODEL] * pd
    gate1 = mod_ref[0, :, 2 * D_MODEL:3 * D_MODEL]
    x = x_ref[...] + gate1 * jnp.dot(merged.astype(BF16), wout_ref[...],
                                     preferred_element_type=F32)

    shift2 = mod_ref[0, :, 3 * D_MODEL:4 * D_MODEL]
    scale2 = mod_ref[0, :, 4 * D_MODEL:5 * D_MODEL]
    gate2 = mod_ref[0, :, 5 * D_MODEL:6 * D_MODEL]
    h = _rms_rows(x) * n2_ref[...]
    h = h * (1.0 + scale2) + shift2
    a = jnp.dot(h.astype(BF16), w1_ref[...], preferred_element_type=F32) + b1_ref[...]
    a = jnp.square(jnp.maximum(a, 0.0))
    ff = jnp.dot(a.astype(BF16), w2_ref[...], preferred_element_type=F32) + b2_ref[...]
    out = x + gate2 * ff
    if final:
        out = _rms_rows(out) * fn_ref[...]
    o_ref[...] = out


def _post(x2d, oa, o_f, o_b, dcat, gates, mod, tiles_per_mod, wpa, wpd, wout, dng, ones_bd, n2, w1,
          b1, w2, b2, final_norm):
    ntok = x2d.shape[0]
    final = final_norm is not None
    row = lambda i: (i, 0)
    fixed = lambda shape: pl.BlockSpec(shape, lambda i: (0, 0), pipeline_mode=pl.Buffered(1))
    in_specs = [
        pl.BlockSpec((TOK_TILE, D_MODEL), row),
        pl.BlockSpec((TOK_TILE, ATTN_W), row),
        pl.BlockSpec((TOK_TILE, DN_W), row),
        pl.BlockSpec((TOK_TILE, DN_W), row),
        pl.BlockSpec((TOK_TILE, DN_W), lambda i: (i, 3)),
        pl.BlockSpec((TOK_TILE, 2 * D_MODEL), row),
        pl.BlockSpec((1, 1, 6 * D_MODEL), lambda i: (i // tiles_per_mod, 0, 0)),
        fixed((ATTN_W, D_MODEL)),
        fixed((DN_W, D_MODEL)),
        fixed((D_MODEL, D_MODEL)),
        fixed((1, DN_W)),
        fixed((GROUP_W, GROUP_W)),
        fixed((1, D_MODEL)),
        fixed((D_MODEL, D_FF)),
        fixed((1, D_FF)),
        fixed((D_FF, D_MODEL)),
        fixed((1, D_MODEL)),
    ]
    args = [x2d, oa, o_f, o_b, dcat, gates, mod, wpa, wpd, wout, dng, ones_bd, n2, w1, b1, w2, b2]
    if final:
        in_specs.append(fixed((1, D_MODEL)))
        args.append(final_norm)
    return pl.pallas_call(
        functools.partial(_post_kernel, final=final),
        grid=(ntok // TOK_TILE,),
        in_specs=in_specs,
        out_specs=pl.BlockSpec((TOK_TILE, D_MODEL), row),
        out_shape=jax.ShapeDtypeStruct((ntok, D_MODEL), F32),
        compiler_params=pltpu.CompilerParams(dimension_semantics=("arbitrary",),
                                             vmem_limit_bytes=VMEM_LIMIT),
        name="merge_mlp",
    )(*args)


def _rope_tables(n_tokens):
    freqs = HEAD // 4
    t = jnp.arange(n_tokens)
    row = (t // GRID_W).astype(F32)
    col = (t % GRID_W).astype(F32)
    inv = ROPE_THETA ** (-jnp.arange(freqs, dtype=F32) / freqs)
    ang = jnp.concatenate([row[:, None] * inv, col[:, None] * inv], axis=-1)
    cos, sin = jnp.cos(ang), jnp.sin(ang)
    cos_h = jnp.concatenate([cos, cos], axis=-1)
    sin_h = jnp.concatenate([-sin, sin], axis=-1)
    reps = LANES // HEAD
    return jnp.tile(cos_h, (1, reps)), jnp.tile(sin_h, (1, reps))


def _pack_w_in(w):
    pad = jnp.zeros((D_MODEL, LANES - 4 * N_DN_HEADS), w.dtype)
    return jnp.concatenate([w[:, :C_AB], w[:, C_AB:C_AB + 4 * N_DN_HEADS], pad,
                            w[:, C_AB + 4 * N_DN_HEADS:]], axis=1).astype(BF16)


def _pad_lanes(v):
    v = v.reshape(1, -1)
    return jnp.pad(v, ((0, 0), (0, LANES - v.shape[1])))


def _expand_selector(direction):
    src = jnp.arange(LANES)[:, None]
    dst = jnp.arange(2 * DN_W)[None, :]
    head = (dst % DN_W) // HEAD
    want = jnp.where(dst < DN_W, direction * N_DN_HEADS + head,
                     2 * N_DN_HEADS + direction * N_DN_HEADS + head)
    return jnp.tile((src == want).astype(BF16), (3, 1))


def _kv_head_orders(k, v):
    k2 = jnp.concatenate([k[:, 0], k[:, 1], k[:, 1], k[:, 0]], axis=-1)
    ones = jnp.ones(v.shape[:1] + v.shape[2:3] + (V_ROWS - HEAD,), v.dtype)
    vt = jnp.concatenate([v[:, 0], ones, v[:, 1], ones], axis=-1)
    return k2.astype(BF16), vt.transpose(0, 2, 1).astype(BF16)


def _state_to_block_diag(s):
    b = s.shape[0]
    s = s.reshape(b, DN_W // GROUP_W, SEGS, HEAD, HEAD).astype(F32)
    eye = jnp.eye(SEGS, dtype=F32)
    bd = s[:, :, :, :, None, :] * eye[None, None, :, None, :, None]
    return bd.reshape(b, DN_W // GROUP_W, GROUP_W, GROUP_W)


def kernel(x_prompt, x_sample, c, cache_k, cache_v, state_delta, c_ctx, w_mod, b_mod, norm1, norm2,
           w_in, conv_w, q_gain, k_gain, a_log, dt_bias, dn_gain, w_pa, w_pd, w_out, w1, b1, w2, b2,
           final_norm):
    depth = w_in.shape[0]
    batch, seq, _ = x_prompt.shape
    dec_batch, dec_seq, _ = x_sample.shape
    assert seq == TOK_TILE and dec_seq % TOK_TILE == 0

    mod_rows = 8
    cvec = jnp.concatenate([c, c_ctx[None, :],
                            jnp.zeros((mod_rows - dec_batch - 1, D_MODEL), F32)], axis=0)
    mod_all = _modulation(cvec, w_mod.astype(BF16), b_mod)

    ones_bd = (jnp.arange(GROUP_W)[:, None] // HEAD == jnp.arange(GROUP_W)[None, :] // HEAD
               ).astype(BF16)
    esel = [_expand_selector(0), _expand_selector(1)]
    rope_tabs = _rope_tables(dec_seq)

    xp = x_prompt.reshape(batch * seq, D_MODEL)
    xs = x_sample.reshape(dec_batch * dec_seq, D_MODEL)
    ks_out, vs_out, st_out = [], [], []
    for l in range(depth):
        w_cat = _pack_w_in(w_in[l])
        wvt = w_in[l][:, ATTN_W + KV_W:ATTN_W + 2 * KV_W].T.astype(BF16)
        n1 =norm1[l].reshape(1, D_MODEL)
        n2 = norm2[l].reshape(1, D_MODEL)
        qg = jnp.tile(q_gain[l], N_Q_HEADS).reshape(1, ATTN_W)
        kg = jnp.tile(k_gain[l], N_KV_HEADS).reshape(1, KV_W)
        dng = jnp.tile(dn_gain[l], N_DN_HEADS).reshape(1, DN_W)
        alog = _pad_lanes(a_log[l])
        dtb = _pad_lanes(dt_bias[l])
        wpa, wpd, wout = w_pa[l].astype(BF16), w_pd[l].astype(BF16), w_out[l].astype(BF16)
        w1b, w2b = w1[l].astype(BF16), w2[l].astype(BF16)
        b1r, b2r = b1[l].reshape(1, D_FF), b2[l].reshape(1, D_MODEL)
        last = l == depth - 1
        fin = final_norm.reshape(1, D_MODEL) if last else None
        mod_lat = mod_all[l, :dec_batch].reshape(dec_batch, 1, 6 * D_MODEL)
        mod_ctx = mod_all[l, dec_batch:dec_batch + 1].reshape(1, 1, 6 * D_MODEL)

        tpm = batch * seq // TOK_TILE
        q, k2, vt, dcat, ab, gates, k_c, v_c = _inproj(xp, mod_ctx, tpm, n1, w_cat, wvt, qg, kg,
                                                       alog, dtb, ones_bd, None, True, seq)
        oa = _attention(q, k2, vt, None, None, batch, seq)
        o_f, qkv, s_f = _deltanet(dcat, ab, conv_w[l], ones_bd, esel[0], None, batch, seq, True, True)
        o_b, s_b = _deltanet(qkv, ab, None, ones_bd, esel[1], None, batch, seq, False, True)
        xp = _post(xp, oa, o_f, o_b, dcat, gates, mod_ctx, tpm, wpa, wpd, wout, dng, ones_bd, n2,
                   w1b, b1r, w2b, b2r, fin)
        ks_out.append(k_c)
        vs_out.append(v_c)
        st_out.append(jnp.stack([s_f, s_b], axis=1))

        tpm = dec_seq // TOK_TILE
        k2c, vtc = _kv_head_orders(cache_k[:, l], cache_v[:, l])
        q, k2, vt, dcat, ab, gates = _inproj(xs, mod_lat, tpm, n1, w_cat, wvt, qg, kg, alog, dtb,
                                             ones_bd, rope_tabs, False, dec_seq)
        oa = _attention(q, k2, vt, k2c, vtc, dec_batch, dec_seq)
        s0f = _state_to_block_diag(state_delta[:, l, 0])
        s0b = _state_to_block_diag(state_delta[:, l, 1])
        o_f, qkv = _deltanet(dcat, ab, conv_w[l], ones_bd, esel[0], s0f, dec_batch, dec_seq, True,
                             False)
        (o_b,) = _deltanet(qkv, ab, None, ones_bd, esel[1], s0b, dec_batch, dec_seq, False, False)
        xs = _post(xs, oa, o_f, o_b, dcat, gates, mod_lat, tpm, wpa, wpd, wout, dng, ones_bd, n2,
                   w1b, b1r, w2b, b2r, fin)

    y_prompt = xp.reshape(batch, seq, D_MODEL)
    y_sample = xs.reshape(dec_batch, dec_seq, D_MODEL)
    return (y_prompt, y_sample, jnp.stack(ks_out, axis=1), jnp.stack(vs_out, axis=1),
            jnp.stack(st_out, axis=1))
```
